```python
import math
import jax, jax.numpy as jnp
from jax import lax
import numpy as np

D_MODEL = 1024
BATCH = 16
SEQ = 4096
DEPTH = 1

HG_HEADS = 4
HG_DK = 128
HG_DV = 128
HG_WIDTH = HG_HEADS * HG_DK
HG_CHUNK = 64
DF_HEADS = 4
DF_DQK = 64
DF_DV = 2 * DF_DQK
DF_QK_WIDTH = DF_HEADS * DF_DQK
DF_V_WIDTH = DF_HEADS * DF_DV
Q_BLOCK = 128
ROPE_THETA = 500000.0
ROT_DIM = DF_DQK // 4
N_MEM = 256
MEM_HEADS = 4
MEM_DH = 128
MEM_WIDTH = MEM_HEADS * MEM_DH
N_BRANCH = 3
IN_WIDTH = 4 * HG_WIDTH + 4 * DF_QK_WIDTH + DF_V_WIDTH + MEM_WIDTH
IN_SPLITS = (512, 1024, 1536, 2048, 2304, 2560, 2816, 3072, 3584)
N_GROUPS = 4
EXPERTS_PER_GROUP = 8
N_EXPERTS = N_GROUPS * EXPERTS_PER_GROUP
TOP_K = 2
D_EXPERT = 512
MOE_BLOCK = 128
DEEPNORM_ALPHA = (2.0 * DEPTH) ** 0.25
DEEPNORM_BETA = (8.0 * DEPTH) ** -0.25
LN_EPS = 1e-5
RMS_EPS = 1e-6

kernel_name = 'hybrid_hgrn2_diffattn_memxattn_hmoe_deepnorm'


def _layer_norm(x, gain, bias):
    xf = x.astype(jnp.float32)
    mu = jnp.mean(xf, axis=-1, keepdims=True)
    var = jnp.mean(jnp.square(xf - mu), axis=-1, keepdims=True)
    y = (xf - mu) * lax.rsqrt(var + LN_EPS)
    return (y * gain.astype(jnp.float32) + bias.astype(jnp.float32)).astype(x.dtype)


def _rms_norm(x, gain):
    xf = x.astype(jnp.float32)
    y = xf * lax.rsqrt(jnp.mean(jnp.square(xf), axis=-1, keepdims=True) + RMS_EPS)
    return (y * gain.astype(jnp.float32)).astype(x.dtype)


def _rope_tables(positions):
    inv_freq = ROPE_THETA ** (-jnp.arange(0, ROT_DIM, 2, dtype=jnp.float32) / ROT_DIM)
    ang = positions.astype(jnp.float32)[..., None] * inv_freq
    return jnp.cos(ang)[:, :, None, :], jnp.sin(ang)[:, :, None, :]


def _apply_partial_rope(t, cos, sin):
    half = ROT_DIM // 2
    t1, t2, rest = t[..., :half], t[..., half:ROT_DIM], t[..., ROT_DIM:]
    c = cos.astype(t.dtype)
    s = sin.astype(t.dtype)
    return jnp.concatenate([t1 * c - t2 * s, t2 * c + t1 * s, rest], axis=-1)


def _hgrn2(q, f_pre, v_in, g, lb, norm_gain):
    b_sz, s_len, _ = q.shape
    n_chunks = s_len // HG_CHUNK
    f32 = jnp.float32
    forget = lb + (1.0 - lb) * jax.nn.sigmoid(f_pre.astype(f32))
    log_f = jnp.log(forget)
    k = 1.0 - forget
    qf = jax.nn.silu(q.astype(f32))
    v = v_in.astype(f32)

    def to_chunks(t):
        return t.reshape(b_sz, n_chunks, HG_CHUNK, HG_HEADS, -1).transpose(1, 0, 3, 2, 4)

    causal = jnp.tril(jnp.ones((HG_CHUNK, HG_CHUNK), dtype=bool))[:, :, None]

    def chunk_step(state, inp):
        qc, kc, vc, lfc = inp
        cum = jnp.cumsum(lfc, axis=2)
        o_inter = jnp.einsum('bhtk,bhkv->bhtv', qc * jnp.exp(cum), state)
        rel = cum[:, :, :, None, :] - cum[:, :, None, :, :]
        decay = jnp.where(causal, jnp.exp(jnp.where(causal, rel, 0.0)), 0.0)
        scores = jnp.einsum('bhtsk,bhsk->bhts', qc[:, :, :, None, :] * decay, kc)
        o_intra = jnp.einsum('bhts,bhsv->bhtv', scores, vc)
        last = cum[:, :, -1:, :]
        state = jnp.exp(last[:, :, 0, :])[..., None] * state + jnp.einsum(
            'bhsk,bhsv->bhkv', kc * jnp.exp(last - cum), vc)
        return state, o_inter + o_intra

    state0 = jnp.zeros((b_sz, HG_HEADS, HG_DK, HG_DV), f32)
    _, o = lax.scan(chunk_step, state0,
                    (to_chunks(qf), to_chunks(k), to_chunks(v), to_chunks(log_f)))
    o = o.transpose(1, 0, 3, 2, 4).reshape(b_sz, s_len, HG_HEADS, HG_DV)
    o = _rms_norm(o, norm_gain.reshape(HG_HEADS, HG_DV))
    gate = jax.nn.silu(g.astype(f32)).reshape(b_sz, s_len, HG_HEADS, HG_DV)
    return (o * gate).reshape(b_sz, s_len, HG_WIDTH).astype(q.dtype)


def _diff_attention(q1, q2, k1, k2, v, lam, lam_init, subln_gain):
    b_sz, s_len = q1.shape[0], q1.shape[1]
    q = jnp.stack([q1, q2], axis=1).transpose(0, 1, 3, 2, 4)
    k = jnp.stack([k1, k2], axis=1).transpose(0, 1, 3, 2, 4)
    vt = v.transpose(0, 2, 1, 3)
    scale = DF_DQK ** -0.5
    key_pos = jnp.arange(s_len)

    def query_block(i):
        start = i * Q_BLOCK
        qb = lax.dynamic_slice_in_dim(q, start, Q_BLOCK, axis=3)
        s = jnp.einsum('bmhqd,bmhkd->bmhqk', qb, k).astype(jnp.float32) * scale
        mask = (start + jnp.arange(Q_BLOCK))[:, None] >= key_pos[None, :]
        p = jax.nn.softmax(jnp.where(mask, s, -jnp.inf), axis=-1)
        w = p[:, 0] - lam * p[:, 1]
        return jnp.einsum('bhqk,bhkd->bhqd', w.astype(vt.dtype), vt)

    o = lax.map(query_block, jnp.arange(s_len // Q_BLOCK))
    o = o.transpose(1, 0, 3, 2, 4).reshape(b_sz, s_len, DF_HEADS, DF_DV)
    o = _rms_norm(o, subln_gain) * (1.0 - lam_init)
    return o.reshape(b_sz, s_len, DF_V_WIDTH)


def _memory_attention(q, k, v):
    s = jnp.einsum('bshd,bnhd->bhsn', q, k).astype(jnp.float32) * (MEM_DH ** -0.5)
    p = jax.nn.softmax(s, axis=-1)
    o = jnp.einsum('bhsn,bnhd->bshd', p.astype(v.dtype), v)
    return o.reshape(o.shape[0], o.shape[1], MEM_WIDTH)


def _token_mixer(x, cos, sin, mem, w_in, w_gates, lb, hg_gain, lam_q1, lam_k1, lam_q2, lam_k2,
                 lam_init, df_gain, w_mem_kv, wb_hg, wb_df, wb_mem, w_out):
    b_sz, s_len, _ = x.shape
    f32 = jnp.float32
    proj = x @ w_in
    hq, hf, hi, hg, dq1, dq2, dk1, dk2, dv, mq = jnp.split(proj, IN_SPLITS, axis=-1)

    y_hg = _hgrn2(hq, hf, hi, hg, lb, hg_gain)

    def heads(t, h):
        return t.reshape(t.shape[0], t.shape[1], h, -1)

    def rope(t):
        return _apply_partial_rope(heads(t, DF_HEADS), cos, sin)

    lam = (jnp.exp(jnp.sum(lam_q1.astype(f32) * lam_k1.astype(f32)))
           - jnp.exp(jnp.sum(lam_q2.astype(f32) * lam_k2.astype(f32))) + lam_init)
    y_df = _diff_attention(rope(dq1), rope(dq2), rope(dk1), rope(dk2), heads(dv, DF_HEADS),
                           lam, lam_init, df_gain)

    mk, mv = jnp.split(mem @ w_mem_kv, 2, axis=-1)
    y_mem = _memory_attention(heads(mq, MEM_HEADS), heads(mk, MEM_HEADS), heads(mv, MEM_HEADS))

    gates = jax.nn.sigmoid((x @ w_gates).astype(f32)).astype(x.dtype)
    gates = gates.reshape(b_sz, s_len, N_BRANCH, D_MODEL)
    merged = (gates[:, :, 0] * (y_hg @ wb_hg) + gates[:, :, 1] * (y_df @ wb_df)
              + gates[:, :, 2] * (y_mem @ wb_mem))
    return merged @ w_out


def _hier_moe(h, w_group_router, w_expert_router, w_gate, w_up, w_down):
    b_sz, s_len, d = h.shape
    n_tok = b_sz * s_len
    hf = h.reshape(n_tok, d)
    f32 = jnp.float32
    group_p = jax.nn.softmax((hf @ w_group_router).astype(f32), axis=-1)
    group_idx = jnp.argmax(group_p, axis=-1).astype(jnp.int32)
    group_w = jnp.max(group_p, axis=-1)
    exp_logits = (hf @ w_expert_router).astype(f32).reshape(n_tok, N_GROUPS, EXPERTS_PER_GROUP)
    in_group = jnp.take_along_axis(exp_logits, group_idx[:, None, None], axis=1)[:, 0]
    top_logit, top_local = lax.top_k(in_group, TOP_K)
    top_w = jax.nn.softmax(top_logit, axis=-1) * group_w[:, None]
    expert_id = (group_idx[:, None] * EXPERTS_PER_GROUP + top_local.astype(jnp.int32)).reshape(-1)

    n_assign = n_tok * TOP_K
    token_id = jnp.repeat(jnp.arange(n_tok, dtype=jnp.int32), TOP_K)
    order = jnp.argsort(expert_id)
    sorted_e = expert_id[order]
    counts = jnp.bincount(expert_id, length=N_EXPERTS).astype(jnp.int32)
    padded = (counts + MOE_BLOCK - 1) // MOE_BLOCK * MOE_BLOCK
    padded_end = jnp.cumsum(padded)
    padded_start = padded_end - padded
    seg_start = jnp.cumsum(counts) - counts
    dest = padded_start[sorted_e] + jnp.arange(n_assign, dtype=jnp.int32) - seg_start[sorted_e]
    n_slots = n_assign + N_EXPERTS * MOE_BLOCK
    n_blocks = n_slots // MOE_BLOCK
    slot_tok = jnp.zeros((n_slots,), jnp.int32).at[dest].set(token_id[order])
    slot_w = jnp.zeros((n_slots,), h.dtype).at[dest].set(top_w.reshape(-1)[order].astype(h.dtype))
    block_start = jnp.arange(n_blocks, dtype=jnp.int32) * MOE_BLOCK
    block_expert = jnp.minimum(jnp.searchsorted(padded_end, block_start, side='right'),
                               N_EXPERTS - 1).astype(jnp.int32)

    def expert_block(args):
        tok, w, e = args
        xb = hf[tok]
        act = jax.nn.silu(xb @ w_gate[e]) * (xb @ w_up[e])
        return (act @ w_down[e]) * w[:, None]

    y = lax.map(expert_block, (slot_tok.reshape(n_blocks, MOE_BLOCK),
                               slot_w.reshape(n_blocks, MOE_BLOCK), block_expert))
    out = jnp.zeros_like(hf).at[slot_tok].add(y.reshape(n_slots, d))
    return out.reshape(b_sz, s_len, d)


def _normal(k, shape, scale):
    return jax.random.normal(k, shape, jnp.float32) * scale


def setup_inputs(seed: int = 0) -> dict:
    key = jax.random.key(seed)
    ks = jax.random.split(key, 32)
    L = DEPTH
    beta = DEEPNORM_BETA
    d_inv = D_MODEL ** -0.5
    offset = jax.random.randint(ks[2], (BATCH, 1), 0, 1024, dtype=jnp.int32)
    positions = (offset + jnp.arange(SEQ, dtype=jnp.int32)[None, :]).astype(jnp.int32)
    return {
        'x': _normal(ks[0], (BATCH, SEQ, D_MODEL), 1.0),
        'mem': _normal(ks[1], (BATCH, N_MEM, D_MODEL), 1.0),
        'positions': positions,
        'w_in': _normal(ks[3], (L, D_MODEL, IN_WIDTH), d_inv),
        'w_gates': _normal(ks[4], (L, D_MODEL, N_BRANCH * D_MODEL), d_inv),
        'hgrn_lower_bounds': _normal(ks[5], (L + 1, HG_WIDTH), 0.1),
        'hgrn_norm_gain': 1.0 + _normal(ks[6], (L, HG_WIDTH), 0.02),
        'diff_lambda_q1': _normal(ks[7], (L, DF_DQK), 0.1),
        'diff_lambda_k1': _normal(ks[8], (L, DF_DQK), 0.1),
        'diff_lambda_q2': _normal(ks[9], (L, DF_DQK), 0.1),
        'diff_lambda_k2': _normal(ks[10], (L, DF_DQK), 0.1),
        'diff_subln_gain': 1.0 + _normal(ks[11], (L, DF_DV), 0.02),
        'w_mem_kv': _normal(ks[12], (L, D_MODEL, 2 * MEM_WIDTH), d_inv),
        'w_branch_hgrn': _normal(ks[13], (L, HG_WIDTH, D_MODEL), beta * HG_WIDTH ** -0.5),
        'w_branch_diff': _normal(ks[14], (L, DF_V_WIDTH, D_MODEL), beta * DF_V_WIDTH ** -0.5),
        'w_branch_mem': _normal(ks[15], (L, MEM_WIDTH, D_MODEL), beta * MEM_WIDTH ** -0.5),
        'w_out': _normal(ks[16], (L, D_MODEL, D_MODEL), beta * d_inv),
        'ln1_gain': 1.0 + _normal(ks[17], (L, D_MODEL), 0.02),
        'ln1_bias': _normal(ks[18], (L, D_MODEL), 0.02),
        'w_group_router': _normal(ks[19], (L, D_MODEL, N_GROUPS), d_inv),
        'w_expert_router': _normal(ks[20], (L, D_MODEL, N_EXPERTS), d_inv),
        'w_expert_gate': _normal(ks[21], (L, N_EXPERTS, D_MODEL, D_EXPERT), d_inv),
        'w_expert_up': _normal(ks[22], (L, N_EXPERTS, D_MODEL, D_EXPERT), d_inv),
        'w_expert_down': _normal(ks[23], (L, N_EXPERTS, D_EXPERT, D_MODEL), beta * D_EXPERT ** -0.5),
        'ln2_gain': 1.0 + _normal(ks[24], (L, D_MODEL), 0.02),
        'ln2_bias': _normal(ks[25], (L, D_MODEL), 0.02),
    }


def reference(x, mem, positions, w_in, w_gates, hgrn_lower_bounds, hgrn_norm_gain,
              diff_lambda_q1, diff_lambda_k1, diff_lambda_q2, diff_lambda_k2, diff_subln_gain,
              w_mem_kv, w_branch_hgrn, w_branch_diff, w_branch_mem, w_out, ln1_gain, ln1_bias,
              w_group_router, w_expert_router, w_expert_gate, w_expert_up, w_expert_down,
              ln2_gain, ln2_bias):
    lower = jnp.cumsum(jax.nn.softmax(hgrn_lower_bounds.astype(jnp.float32), axis=0), axis=0)
    cos, sin = _rope_tables(positions)
    for l in range(DEPTH):
        lam_init = 0.8 - 0.6 * math.exp(-0.3 * l)
        mix = _token_mixer(x, cos, sin, mem, w_in[l], w_gates[l], lower[l], hgrn_norm_gain[l],
                           diff_lambda_q1[l], diff_lambda_k1[l], diff_lambda_q2[l], diff_lambda_k2[l],
                           lam_init, diff_subln_gain[l], w_mem_kv[l], w_branch_hgrn[l],
                           w_branch_diff[l], w_branch_mem[l], w_out[l])
        x = _layer_norm(DEEPNORM_ALPHA * x + mix, ln1_gain[l], ln1_bias[l])
        ffn = _hier_moe(x, w_group_router[l], w_expert_router[l], w_expert_gate[l],
                        w_expert_up[l], w_expert_down[l])
        x = _layer_norm(DEEPNORM_ALPHA * x + ffn, ln2_gain[l], ln2_bias[l])
    return x
```

```python
import functools
import math

import jax
import jax.numpy as jnp
from jax import lax
from jax.experimental import pallas as pl
from jax.experimental.pallas import tpu as pltpu

F32 = jnp.float32
BF16 = jnp.bfloat16

HG_HEADS = 4
HG_DK = 128
HG_CHUNK = 64
DF_HEADS = 4
DF_DQK = 64
DF_DV = 128
ROPE_THETA = 500000.0
ROT_DIM = 16
MEM_HEADS = 4
MEM_DH = 128
N_BRANCH = 3
N_GROUPS = 4
EXPERTS_PER_GROUP = 8
N_EXPERTS = 32
TOP_K = 2
DEPTH = 1
DEEPNORM_ALPHA = (2.0 * DEPTH) ** 0.25
LN_EPS = 1e-5
RMS_EPS = 1e-6
LANES = 128
V7X_VMEM_LIMIT = 56 * 1024 * 1024


def _tiles(seq, n_tok):
    return dict(
        proj_rows=min(512, n_tok),
        hgrn_rows=min(512, seq),
        attn_q=min(512, seq),
        attn_k=min(512, seq),
        mem_q=min(512, seq),
        merge_rows=min(256, n_tok),
        scatter_rows=min(1024, n_tok),
        moe_rows=256,
        combine_rows=min(256, n_tok),
    )


def _sigmoid(v):
    return 1.0 / (1.0 + jnp.exp(-v))


def _dot(a, b):
    return jnp.dot(a, b, preferred_element_type=F32)


def _dot_nt(a, b):
    return lax.dot_general(a, b, (((1,), (1,)), ((), ())), preferred_element_type=F32)


def _dot_tn(a, b):
    return lax.dot_general(a, b, (((0,), (0,)), ((), ())), preferred_element_type=F32)


def _proj_kernel(x_ref, pos_ref, w_ref, lbraw_ref, invf_ref, p_ref, logf_ref, *, layer):
    tm = x_ref.shape[0]
    wd = logf_ref.shape[1]
    x = x_ref[...].astype(BF16)

    def mm(j):
        return _dot(x, w_ref[:, j * wd:(j + 1) * wd])

    a = lbraw_ref[...]
    e = jnp.exp(a - jnp.max(a, axis=0, keepdims=True))
    sm = e / jnp.sum(e, axis=0, keepdims=True)
    lb = jnp.sum(sm[0:layer + 1, :], axis=0, keepdims=True)

    hq = mm(0)
    p_ref[:, 0:wd] = (hq * _sigmoid(hq)).astype(BF16)
    hf = mm(1)
    forget = lb + (1.0 - lb) * _sigmoid(hf)
    logf_ref[...] = jnp.log(forget)
    p_ref[:, wd:2 * wd] = (1.0 - forget).astype(BF16)
    p_ref[:, 2 * wd:3 * wd] = mm(2).astype(BF16)
    hg = mm(3)
    p_ref[:, 3 * wd:4 * wd] = (hg * _sigmoid(hg)).astype(BF16)

    ang = invf_ref[...] * pos_ref[...].astype(F32)
    c8 = jnp.cos(ang)
    s8 = jnp.sin(ang)
    one = jnp.ones((DF_DQK - ROT_DIM, tm), F32)
    zero = jnp.zeros((DF_DQK - ROT_DIM, tm), F32)
    z8 = jnp.zeros_like(s8)
    cos_t = jnp.concatenate([c8, c8, one, c8, c8, one], axis=0).T
    sin_lo = jnp.concatenate([-s8, z8, zero, -s8, z8, zero], axis=0).T
    sin_hi = jnp.concatenate([z8, s8, zero, z8, s8, zero], axis=0).T
    half = ROT_DIM // 2

    def rope(t):
        return (t * cos_t + pltpu.roll(t, LANES - half, 1) * sin_lo
                + pltpu.roll(t, half, 1) * sin_hi)

    q = mm(4)
    k = mm(5)
    for j in range(wd // LANES):
        sl = slice(j * LANES, (j + 1) * LANES)
        p_ref[:, 4 * wd + j * LANES:4 * wd + (j + 1) * LANES] = (
            rope(q[:, sl]) * (DF_DQK ** -0.5)).astype(BF16)
        p_ref[:, 5 * wd + j * LANES:5 * wd + (j + 1) * LANES] = rope(k[:, sl]).astype(BF16)
    p_ref[:, 6 * wd:7 * wd] = mm(6).astype(BF16)
    p_ref[:, 7 * wd:8 * wd] = mm(7).astype(BF16)


def _proj(x2, pos3, w_in_b, lbraw, invf8, *, tm, layer):
    n_tok, d = x2.shape
    width = w_in_b.shape[1]
    wd = lbraw.shape[1]
    return pl.pallas_call(
        functools.partial(_proj_kernel, layer=layer),
        out_shape=(jax.ShapeDtypeStruct((n_tok, width), BF16),
                   jax.ShapeDtypeStruct((n_tok, wd), F32)),
        grid=(n_tok // tm,),
        in_specs=[
            pl.BlockSpec((tm, d), lambda i: (i, 0)),
            pl.BlockSpec((None, 1, tm), lambda i: (i, 0, 0)),
            pl.BlockSpec((d, width), lambda i: (0, 0)),
            pl.BlockSpec(lbraw.shape, lambda i: (0, 0)),
            pl.BlockSpec(invf8.shape, lambda i: (0, 0)),
        ],
        out_specs=(pl.BlockSpec((tm, width), lambda i: (i, 0)),
                   pl.BlockSpec((tm, wd), lambda i: (i, 0))),
        compiler_params=pltpu.CompilerParams(
            dimension_semantics=("arbitrary",), vmem_limit_bytes=V7X_VMEM_LIMIT),
        name="proj",
    )(x2, pos3, w_in_b, lbraw, invf8)


def _hgrn_kernel(q_ref, k_ref, v_ref, g_ref, lf_ref, gain_ref, lvl_ref, tri_ref, o_ref, st_ref):
    rows = q_ref.shape[0]
    c = HG_CHUNK
    dk = HG_DK

    @pl.when(pl.program_id(1) == 0)
    def _():
        st_ref[...] = jnp.zeros_like(st_ref)

    lvl = lvl_ref[...]
    tri = tri_ref[...]
    sub = lax.broadcasted_iota(jnp.int32, (c // 8, 8, dk), 1)
    row = lax.broadcasted_iota(jnp.int32, (c, dk), 0)

    def anchors(cum):
        out = []
        for m in (32, 16, 8):
            pieces = []
            for j in range(c // (2 * m)):
                a = j * 2 * m + m - 1
                pieces.append(jnp.broadcast_to(cum[a:a + 1, :], (2 * m, dk)))
            out.append(pieces[0] if len(pieces) == 1 else jnp.concatenate(pieces, axis=0))
        c8 = cum.reshape(c // 8, 8, dk)
        out.append(jnp.broadcast_to(c8[:, 3:4, :], c8.shape).reshape(c, dk))
        a2 = jnp.where(sub < 4, jnp.broadcast_to(c8[:, 1:2, :], c8.shape),
                       jnp.broadcast_to(c8[:, 5:6, :], c8.shape))
        out.append(a2.reshape(c, dk))
        return out

    def chunk_body(ci, carry):
        r0 = pl.multiple_of(ci * c, c)
        for h in range(HG_HEADS):
            hs = slice(h * dk, (h + 1) * dk)
            q = q_ref[pl.ds(r0, c), hs].astype(F32)
            k = k_ref[pl.ds(r0, c), hs].astype(F32)
            v = v_ref[pl.ds(r0, c), hs]
            lf = lf_ref[pl.ds(r0, c), hs]
            lf1 = lf.astype(BF16)
            r1 = lf - lf1.astype(F32)
            lf2 = r1.astype(BF16)
            lf3 = (r1 - lf2.astype(F32)).astype(BF16)
            cum = _dot(tri, lf1) + _dot(tri, lf2) + _dot(tri, lf3)
            e_lvls = [jnp.exp(-jnp.abs(cum - a)) for a in anchors(cum)]
            e_lvls.append(jnp.where((row & 1) == 1, jnp.exp(lf), 1.0))
            scores = jnp.where(lvl == 6, _dot_nt(q.astype(BF16), k.astype(BF16)), 0.0)
            for i, e in enumerate(e_lvls):
                s_i = _dot_nt((q * e).astype(BF16), (k * e).astype(BF16))
                scores = jnp.where(lvl == i, s_i, scores)
            o = _dot(scores.astype(BF16), v)
            ecum = jnp.exp(cum)
            st = st_ref[h]
            o = o + _dot_nt((q * ecum).astype(BF16), st.astype(BF16))
            last = cum[c - 1:c, :]
            k_dec = (k * jnp.exp(last - cum)).astype(BF16)
            st_ref[h] = st * ecum[c - 1:c, :] + _dot_tn(v, k_dec)
            y = o * lax.rsqrt(jnp.mean(o * o, axis=-1, keepdims=True) + RMS_EPS)
            y = y * gain_ref[:, hs] * g_ref[pl.ds(r0, c), hs].astype(F32)
            o_ref[pl.ds(r0, c), hs] = y.astype(o_ref.dtype)
        return carry

    lax.fori_loop(0, rows // c, chunk_body, 0)


def _hgrn_consts():
    c = HG_CHUNK
    t = jnp.arange(c)[:, None]
    s = jnp.arange(c)[None, :]
    x = t ^ s
    lvl = jnp.full((c, c), -1, jnp.int32)
    for i, m in enumerate((32, 16, 8, 4, 2, 1)):
        lvl = jnp.where((t > s) & (x >= m) & (x < 2 * m), i, lvl)
    lvl = jnp.where(t == s, 6, lvl)
    tri = (t >= s).astype(BF16)
    return lvl, tri


def _hgrn(p3, logf3, gain, *, rows):
    b, s, _ = p3.shape
    wd = logf3.shape[2]
    lvl, tri = _hgrn_consts()

    def col(j):
        return pl.BlockSpec((None, rows, wd), lambda bi, si: (bi, si, j))

    return pl.pallas_call(
        _hgrn_kernel,
        out_shape=jax.ShapeDtypeStruct((b, s, wd), BF16),
        grid=(b, s // rows),
        in_specs=[col(0), col(1), col(2), col(3), col(0),
                  pl.BlockSpec(gain.shape, lambda bi, si: (0, 0)),
                  pl.BlockSpec(lvl.shape, lambda bi, si: (0, 0)),
                  pl.BlockSpec(tri.shape, lambda bi, si: (0, 0))],
        out_specs=col(0),
        scratch_shapes=[pltpu.VMEM((HG_HEADS, HG_DK, HG_DK), F32)],
        compiler_params=pltpu.CompilerParams(
            dimension_semantics=("arbitrary", "arbitrary"), vmem_limit_bytes=V7X_VMEM_LIMIT),
        name="hgrn",
    )(p3, p3, p3, p3, logf3, gain, lvl, tri)


def _diffattn_kernel(q_ref, k_ref, v_ref, lq1_ref, lk1_ref, lq2_ref, lk2_ref, gain_ref, o_ref,
                     m_ref, l_ref, acc_ref, *, tk, lam_init):
    tq = q_ref.shape[0]
    qi = pl.program_id(2)
    lane = lax.broadcasted_iota(jnp.int32, (tq, LANES), 1)
    q = q_ref[...]
    zero = jnp.zeros_like(q)
    qm = (jnp.where(lane < DF_DQK, q, zero), jnp.where(lane >= DF_DQK, q, zero))

    m_ref[...] = jnp.full(m_ref.shape, -jnp.inf, F32)
    l_ref[...] = jnp.zeros(l_ref.shape, F32)
    acc_ref[...] = jnp.zeros(acc_ref.shape, F32)

    def step(k0, masked):
        kt = k_ref[pl.ds(k0, tk), :]
        vt = v_ref[pl.ds(k0, tk), :]
        if masked:
            r = lax.broadcasted_iota(jnp.int32, (tq, tk), 0) + qi * tq
            cidx = lax.broadcasted_iota(jnp.int32, (tq, tk), 1) + k0
            keep = r >= cidx
        for j in range(2):
            s = _dot_nt(qm[j], kt)
            if masked:
                s = jnp.where(keep, s, -jnp.inf)
            m_old = m_ref[j]
            m_new = jnp.maximum(m_old, jnp.max(s, axis=-1, keepdims=True))
            alpha = jnp.exp(m_old - m_new)
            p = jnp.exp(s - m_new)
            l_ref[j] = alpha * l_ref[j] + jnp.sum(p, axis=-1, keepdims=True)
            acc_ref[j] = alpha * acc_ref[j] + _dot(p.astype(BF16), vt)
            m_ref[j] = m_new

    def body(ki, carry):
        step(pl.multiple_of(ki * tk, tk), False)
        return carry

    n_full = (qi * tq) // tk
    lax.fori_loop(0, n_full, body, 0)
    for d in range(tq // tk):
        step(pl.multiple_of(qi * tq + d * tk, tk), True)

    lam = (jnp.exp(jnp.sum(lq1_ref[...] * lk1_ref[...], keepdims=True))
           - jnp.exp(jnp.sum(lq2_ref[...] * lk2_ref[...], keepdims=True)) + lam_init)
    o = acc_ref[0] / l_ref[0] - lam * (acc_ref[1] / l_ref[1])
    y = o * lax.rsqrt(jnp.mean(o * o, axis=-1, keepdims=True) + RMS_EPS)
    o_ref[...] = (y * gain_ref[...] * (1.0 - lam_init)).astype(o_ref.dtype)


def _diffattn(p3, lq1, lk1, lq2, lk2, gain, *, tq, tk, lam_init, col0):
    b, s, _ = p3.shape
    assert tq % tk == 0
    small = pl.BlockSpec((1, DF_DQK), lambda bi, hi, qi: (0, 0))
    return pl.pallas_call(
        functools.partial(_diffattn_kernel, tk=tk, lam_init=lam_init),
        out_shape=jax.ShapeDtypeStruct((b, s, DF_HEADS * DF_DV), BF16),
        grid=(b, DF_HEADS, s // tq),
        in_specs=[
            pl.BlockSpec((None, tq, LANES), lambda bi, hi, qi: (bi, qi, col0 + hi)),
            pl.BlockSpec((None, s, LANES), lambda bi, hi, qi: (bi, 0, col0 + DF_HEADS + hi)),
            pl.BlockSpec((None, s, LANES), lambda bi, hi, qi: (bi, 0, col0 + 2 * DF_HEADS + hi)),
            small, small, small, small,
            pl.BlockSpec((1, DF_DV), lambda bi, hi, qi: (0, 0)),
        ],
        out_specs=pl.BlockSpec((None, tq, DF_DV), lambda bi, hi, qi: (bi, qi, hi)),
        scratch_shapes=[pltpu.VMEM((2, tq, 1), F32), pltpu.VMEM((2, tq, 1), F32),
                        pltpu.VMEM((2, tq, DF_DV), F32)],
        compiler_params=pltpu.CompilerParams(
            dimension_semantics=("arbitrary", "arbitrary", "arbitrary"),
            vmem_limit_bytes=V7X_VMEM_LIMIT),
        name="diffattn",
    )(p3, p3, p3, lq1, lk1, lq2, lk2, gain)


def _memattn_kernel(q_ref, mem_ref, w_ref, o_ref, kv_ref):
    width = MEM_HEADS * MEM_DH

    @pl.when(pl.program_id(1) == 0)
    def _():
        kv_ref[...] = _dot(mem_ref[...].astype(BF16), w_ref[...]).astype(BF16)

    for h in range(MEM_HEADS):
        hs = slice(h * MEM_DH, (h + 1) * MEM_DH)
        s = _dot_nt(q_ref[:, hs], kv_ref[:, hs]) * (MEM_DH ** -0.5)
        e = jnp.exp(s - jnp.max(s, axis=-1, keepdims=True))
        p = e / jnp.sum(e, axis=-1, keepdims=True)
        o_ref[:, hs] = _dot(p.astype(BF16), kv_ref[:, width + h * MEM_DH:width + (h + 1) * MEM_DH]
                            ).astype(o_ref.dtype)


def _memattn(p3, mem, w_kv_b, *, tq, col0):
    b, s, _ = p3.shape
    n_mem, d = mem.shape[1], mem.shape[2]
    width = MEM_HEADS * MEM_DH
    return pl.pallas_call(
        _memattn_kernel,
        out_shape=jax.ShapeDtypeStruct((b, s, width), BF16),
        grid=(b, s // tq),
        in_specs=[
            pl.BlockSpec((None, tq, width), lambda bi, qi: (bi, qi, col0)),
            pl.BlockSpec((None, n_mem, d), lambda bi, qi: (bi, 0, 0)),
            pl.BlockSpec(w_kv_b.shape, lambda bi, qi: (0, 0)),
        ],
        out_specs=pl.BlockSpec((None, tq, width), lambda bi, qi: (bi, qi, 0)),
        scratch_shapes=[pltpu.VMEM((n_mem, 2 * width), BF16)],
        compiler_params=pltpu.CompilerParams(
            dimension_semantics=("arbitrary", "arbitrary"), vmem_limit_bytes=V7X_VMEM_LIMIT),
        name="memattn",
    )(p3, mem, w_kv_b)


def _layer_norm(z, gain, bias):
    mu = jnp.mean(z, axis=-1, keepdims=True)
    zc = z - mu
    var = jnp.mean(zc * zc, axis=-1, keepdims=True)
    return zc * lax.rsqrt(var + LN_EPS) * gain + bias


def _merge_kernel(x_ref, yh_ref, yd_ref, ym_ref, wg_ref, wbh_ref, wbd_ref, wbm_ref, wo_ref,
                  g1_ref, b1_ref, wr_hi_ref, wr_lo_ref, tri_ref, h_ref, meta_ref, cnt_ref, run_ref):
    tm, d = x_ref.shape

    @pl.when(pl.program_id(0) == 0)
    def _():
        run_ref[...] = jnp.zeros_like(run_ref)

    x = x_ref[...]
    xb = x.astype(BF16)
    merged = None
    for j, (y_ref, wb_ref) in enumerate(((yh_ref, wbh_ref), (yd_ref, wbd_ref), (ym_ref, wbm_ref))):
        gate = _sigmoid(_dot(xb, wg_ref[:, j * d:(j + 1) * d]))
        term = gate * _dot(y_ref[...], wb_ref[...])
        merged = term if merged is None else merged + term
    mix = _dot(merged.astype(BF16), wo_ref[...])
    h = _layer_norm(DEEPNORM_ALPHA * x + mix, g1_ref[...], b1_ref[...])
    h_ref[...] = h

    h_hi = h.astype(BF16)
    h_lo = (h - h_hi.astype(F32)).astype(BF16)
    lg = _dot(h_hi, wr_hi_ref[...]) + _dot(h_hi, wr_lo_ref[...]) + _dot(h_lo, wr_hi_ref[...])

    lane = lax.broadcasted_iota(jnp.int32, (tm, LANES), 1)
    neg = jnp.full_like(lg, -jnp.inf)
    big = jnp.full_like(lane, 4 * LANES)
    is_g = lane < N_GROUPS
    g_max = jnp.max(jnp.where(is_g, lg, neg), axis=-1, keepdims=True)
    g_sum = jnp.sum(jnp.where(is_g, jnp.exp(lg - g_max), 0.0), axis=-1, keepdims=True)
    group_w = 1.0 / g_sum
    g_idx = jnp.min(jnp.where(is_g & (lg == g_max), lane, big), axis=-1, keepdims=True)
    in_grp = ((lane >= N_GROUPS) & (lane < N_GROUPS + N_EXPERTS)
              & (jnp.right_shift(lane - N_GROUPS, 3) == g_idx))
    v1 = jnp.max(jnp.where(in_grp, lg, neg), axis=-1, keepdims=True)
    i1 = jnp.min(jnp.where(in_grp & (lg == v1), lane, big), axis=-1, keepdims=True)
    rest = in_grp & (lane != i1)
    v2 = jnp.max(jnp.where(rest, lg, neg), axis=-1, keepdims=True)
    i2 = jnp.min(jnp.where(rest & (lg == v2), lane, big), axis=-1, keepdims=True)
    e21 = jnp.exp(v2 - v1)
    w1 = group_w / (1.0 + e21)
    w2 = group_w * e21 / (1.0 + e21)

    hot1 = lane == i1
    hot2 = lane == i2
    hot = jnp.where(hot1 | hot2, 1.0, 0.0)
    before = _dot(tri_ref[...], hot.astype(BF16)) + run_ref[...]
    r1 = jnp.sum(jnp.where(hot1, before, 0.0), axis=-1, keepdims=True)
    r2 = jnp.sum(jnp.where(hot2, before, 0.0), axis=-1, keepdims=True)
    run_new = before[tm - 1:tm, :] + hot[tm - 1:tm, :]
    run_ref[...] = run_new
    cnt_ref[...] = run_new

    def put(col, val, acc):
        return jnp.where(lane == col, val, acc)

    meta = jnp.zeros((tm, LANES), F32)
    meta = put(0, (i1 - N_GROUPS).astype(F32), meta)
    meta = put(1, (i2 - N_GROUPS).astype(F32), meta)
    meta = put(2, r1, meta)
    meta = put(3, r2, meta)
    meta = put(4, w1, meta)
    meta = put(5, w2, meta)
    meta_ref[...] = meta


def _merge(x2, yh, yd, ym, wg_b, wbh_b, wbd_b, wbm_b, wo_b, g1, b1, wr_hi, wr_lo, *, tm):
    n_tok, d = x2.shape
    tri = (jnp.arange(tm)[:, None] > jnp.arange(tm)[None, :]).astype(BF16)
    row = lambda w: pl.BlockSpec((tm, w), lambda i: (i, 0))
    full = lambda a: pl.BlockSpec(a.shape, lambda i: (0, 0))
    return pl.pallas_call(
        _merge_kernel,
        out_shape=(jax.ShapeDtypeStruct((n_tok, d), F32),
                   jax.ShapeDtypeStruct((n_tok, LANES), F32),
                   jax.ShapeDtypeStruct((1, LANES), F32)),
        grid=(n_tok // tm,),
        in_specs=[row(d), row(yh.shape[1]), row(yd.shape[1]), row(ym.shape[1]),
                  full(wg_b), full(wbh_b), full(wbd_b), full(wbm_b), full(wo_b),
                  full(g1), full(b1), full(wr_hi), full(wr_lo), full(tri)],
        out_specs=(row(d), row(LANES), pl.BlockSpec((1, LANES), lambda i: (0, 0))),
        scratch_shapes=[pltpu.VMEM((1, LANES), F32)],
        compiler_params=pltpu.CompilerParams(
            dimension_semantics=("arbitrary",), vmem_limit_bytes=V7X_VMEM_LIMIT),
        name="merge",
    )(x2, yh, yd, ym, wg_b, wbh_b, wbd_b, wbm_b, wo_b, g1, b1, wr_hi, wr_lo, tri)


def _scatter_kernel(dest_ref, h_hbm, xs_in_hbm, xs_hbm, dest_smem, sem, *, ts):
    del xs_in_hbm
    i = pl.program_id(0)
    cp = pltpu.make_async_copy(dest_ref, dest_smem, sem.at[1])
    cp.start()
    cp.wait()

    def issue(t, carry):
        row = i * ts + t
        for k in range(TOP_K):
            d = dest_smem[0, 0, TOP_K * t + k]
            pltpu.make_async_copy(h_hbm.at[pl.ds(row, 1)], xs_hbm.at[pl.ds(d, 1)], sem.at[0]).start()
        return carry

    lax.fori_loop(0, ts, issue, 0)
    pltpu.make_async_copy(xs_hbm.at[pl.ds(0, TOP_K * ts)], xs_hbm.at[pl.ds(0, TOP_K * ts)],
                          sem.at[0]).wait()


def _scatter(dest3, h, xs_zero, *, ts):
    n_tok, d = h.shape
    return pl.pallas_call(
        functools.partial(_scatter_kernel, ts=ts),
        out_shape=jax.ShapeDtypeStruct(xs_zero.shape, xs_zero.dtype),
        grid=(n_tok // ts,),
        in_specs=[pl.BlockSpec((1, 1, TOP_K * ts), lambda i: (i, 0, 0)),
                  pl.BlockSpec(memory_space=pl.ANY),
                  pl.BlockSpec(memory_space=pl.ANY)],
        out_specs=pl.BlockSpec(memory_space=pl.ANY),
        scratch_shapes=[pltpu.SMEM((1, 1, TOP_K * ts), jnp.int32), pltpu.SemaphoreType.DMA((2,))],
        input_output_aliases={2: 0},
        compiler_params=pltpu.CompilerParams(dimension_semantics=("arbitrary",)),
        name="scatter",
    )(dest3, h, xs_zero)


def _experts_kernel(be_ref, nv_ref, x_ref, wg_ref, wu_ref, wd_ref, y_ref):
    j = pl.program_id(0)

    @pl.when(j < nv_ref[0])
    def _():
        x = x_ref[...].astype(BF16)
        g = _dot(x, wg_ref[...])
        u = _dot(x, wu_ref[...])
        act = g * _sigmoid(g) * u
        y_ref[...] = _dot(act.astype(BF16), wd_ref[...])

    @pl.when(j >= nv_ref[0])
    def _():
        y_ref[...] = jnp.zeros_like(y_ref)


def _experts(block_expert, n_valid, xs, wg_b, wu_b, wd_b, *, rows):
    n_slots, d = xs.shape
    de = wg_b.shape[2]
    return pl.pallas_call(
        _experts_kernel,
        out_shape=jax.ShapeDtypeStruct((n_slots, d), F32),
        grid_spec=pltpu.PrefetchScalarGridSpec(
            num_scalar_prefetch=2,
            grid=(n_slots // rows,),
            in_specs=[pl.BlockSpec((rows, d), lambda j, be, nv: (j, 0)),
                      pl.BlockSpec((None, d, de), lambda j, be, nv: (be[j], 0, 0)),
                      pl.BlockSpec((None, d, de), lambda j, be, nv: (be[j], 0, 0)),
                      pl.BlockSpec((None, de, d), lambda j, be, nv: (be[j], 0, 0))],
            out_specs=pl.BlockSpec((rows, d), lambda j, be, nv: (j, 0)),
        ),
        compiler_params=pltpu.CompilerParams(
            dimension_semantics=("arbitrary",), vmem_limit_bytes=V7X_VMEM_LIMIT),
        name="experts",
    )(block_expert, n_valid, xs, wg_b, wu_b, wd_b)


def _combine_kernel(dest_ref, h_ref, meta_ref, g2_ref, b2_ref, y_hbm, o_ref, dest_smem, ybuf, sem,
                    *, tc):
    cp = pltpu.make_async_copy(dest_ref, dest_smem, sem.at[1])
    cp.start()
    cp.wait()

    def issue(t, carry):
        for k in range(TOP_K):
            d = dest_smem[0, 0, TOP_K * t + k]
            pltpu.make_async_copy(y_hbm.at[pl.ds(d, 1)], ybuf.at[k, pl.ds(t, 1)], sem.at[0]).start()
        return carry

    lax.fori_loop(0, tc, issue, 0)
    for k in range(TOP_K):
        pltpu.make_async_copy(y_hbm.at[pl.ds(0, tc)], ybuf.at[k], sem.at[0]).wait()

    meta = meta_ref[...]
    ffn = meta[:, 4:5] * ybuf[0] + meta[:, 5:6] * ybuf[1]
    o_ref[...] = _layer_norm(DEEPNORM_ALPHA * h_ref[...] + ffn, g2_ref[...], b2_ref[...])


def _combine(dest3, h, meta, g2, b2, y, *, tc):
    n_tok, d = h.shape
    return pl.pallas_call(
        functools.partial(_combine_kernel, tc=tc),
        out_shape=jax.ShapeDtypeStruct((n_tok, d), F32),
        grid=(n_tok // tc,),
        in_specs=[pl.BlockSpec((1, 1, TOP_K * tc), lambda i: (i, 0, 0)),
                  pl.BlockSpec((tc, d), lambda i: (i, 0)),
                  pl.BlockSpec((tc, LANES), lambda i: (i, 0)),
                  pl.BlockSpec(g2.shape, lambda i: (0, 0)),
                  pl.BlockSpec(b2.shape, lambda i: (0, 0)),
                  pl.BlockSpec(memory_space=pl.ANY)],
        out_specs=pl.BlockSpec((tc, d), lambda i: (i, 0)),
        scratch_shapes=[pltpu.SMEM((1, 1, TOP_K * tc), jnp.int32),
                        pltpu.VMEM((TOP_K, tc, d), F32),
                        pltpu.SemaphoreType.DMA((2,))],
        compiler_params=pltpu.CompilerParams(
            dimension_semantics=("arbitrary",), vmem_limit_bytes=V7X_VMEM_LIMIT),
        name="combine",
    )(dest3, h, meta, g2, b2, y)


def _w_in_column_order():
    hgw = HG_HEADS * HG_DK
    qkw = DF_HEADS * DF_DQK
    base = 4 * hgw
    cols = list(range(base))
    for first in (base, base + 2 * qkw):
        for h in range(DF_HEADS):
            cols += list(range(first + h * DF_DQK, first + (h + 1) * DF_DQK))
            cols += list(range(first + qkw + h * DF_DQK, first + qkw + (h + 1) * DF_DQK))
    total = base + 4 * qkw + DF_HEADS * DF_DV + MEM_HEADS * MEM_DH
    cols += list(range(base + 4 * qkw, total))
    return jnp.asarray(cols, jnp.int32)


def kernel(x, mem, positions, w_in, w_gates, hgrn_lower_bounds, hgrn_norm_gain, diff_lambda_q1, diff_lambda_k1, diff_lambda_q2, diff_lambda_k2, diff_subln_gain, w_mem_kv, w_branch_hgrn, w_branch_diff, w_branch_mem, w_out, ln1_gain, ln1_bias, w_group_router, w_expert_router, w_expert_gate, w_expert_up, w_expert_down, ln2_gain, ln2_bias):
    b, s, d = x.shape
    n_tok = b * s
    t = _tiles(s, n_tok)
    assert w_in.shape[0] == DEPTH
    layer = 0
    lam_init = 0.8 - 0.6 * math.exp(-0.3 * layer)
    hgw = HG_HEADS * HG_DK

    w_in_b = jnp.take(w_in[layer], _w_in_column_order(), axis=1).astype(BF16)
    inv_freq = ROPE_THETA ** (-jnp.arange(0, ROT_DIM, 2, dtype=F32) / ROT_DIM)
    invf8 = inv_freq.reshape(ROT_DIM // 2, 1)
    w_router = jnp.concatenate(
        [w_group_router[layer], w_expert_router[layer],
         jnp.zeros((d, LANES - N_GROUPS - N_EXPERTS), F32)], axis=1)
    wr_hi = w_router.astype(BF16)
    wr_lo = (w_router - wr_hi.astype(F32)).astype(BF16)

    x2 = x.reshape(n_tok, d)
    pos3 = positions.reshape(n_tok // t["proj_rows"], 1, t["proj_rows"])
    p, logf = _proj(x2, pos3, w_in_b, hgrn_lower_bounds, invf8, tm=t["proj_rows"], layer=layer)
    p3 = p.reshape(b, s, p.shape[1])

    y_hg = _hgrn(p3, logf.reshape(b, s, hgw), hgrn_norm_gain[layer].reshape(1, hgw),
                 rows=t["hgrn_rows"])
    y_df = _diffattn(p3, diff_lambda_q1[layer].reshape(1, -1), diff_lambda_k1[layer].reshape(1, -1),
                     diff_lambda_q2[layer].reshape(1, -1), diff_lambda_k2[layer].reshape(1, -1),
                     diff_subln_gain[layer].reshape(1, -1), tq=t["attn_q"], tk=t["attn_k"],
                     lam_init=lam_init, col0=4 * hgw // LANES)
    y_mem = _memattn(p3, mem, w_mem_kv[layer].astype(BF16), tq=t["mem_q"],
                     col0=7 * hgw // (MEM_HEADS * MEM_DH))

    h, meta, counts_f = _merge(
        x2, y_hg.reshape(n_tok, -1), y_df.reshape(n_tok, -1), y_mem.reshape(n_tok, -1),
        w_gates[layer].astype(BF16), w_branch_hgrn[layer].astype(BF16),
        w_branch_diff[layer].astype(BF16), w_branch_mem[layer].astype(BF16),
        w_out[layer].astype(BF16), ln1_gain[layer].reshape(1, d), ln1_bias[layer].reshape(1, d),
        wr_hi, wr_lo, tm=t["merge_rows"])

    rows = t["moe_rows"]
    counts = counts_f[0, N_GROUPS:N_GROUPS + N_EXPERTS].astype(jnp.int32)
    padded = (counts + rows - 1) // rows * rows
    padded_end = jnp.cumsum(padded)
    padded_start = padded_end - padded
    expert_id = meta[:, 0:TOP_K].astype(jnp.int32)
    rank = meta[:, TOP_K:2 * TOP_K].astype(jnp.int32)
    dest = padded_start[expert_id] + rank
    n_slots = n_tok * TOP_K + N_EXPERTS * rows
    n_blocks = n_slots // rows
    block_start = jnp.arange(n_blocks, dtype=jnp.int32) * rows
    block_expert = jnp.minimum(jnp.searchsorted(padded_end, block_start, side="right"),
                               N_EXPERTS - 1).astype(jnp.int32)
    n_valid = (padded_end[-1:] // rows).astype(jnp.int32)

    ts = t["scatter_rows"]
    xs = _scatter(dest.reshape(n_tok // ts, 1, TOP_K * ts), h, jnp.zeros((n_slots, d), F32), ts=ts)
    y = _experts(block_expert, n_valid, xs, w_expert_gate[layer].astype(BF16),
                 w_expert_up[layer].astype(BF16), w_expert_down[layer].astype(BF16), rows=rows)
    tc = t["combine_rows"]
    out = _combine(dest.reshape(n_tok // tc, 1, TOP_K * tc), h, meta,
                   ln2_gain[layer].reshape(1, d), ln2_bias[layer].reshape(1, d), y, tc=tc)
    return out.reshape(b, s, d)
```

```python
import functools
import math

import jax
import jax.numpy as jnp
from jax import lax
from jax.experimental import pallas as pl
from jax.experimental.pallas import tpu as pltpu

F32 = jnp.float32
BF16 = jnp.bfloat16

HG_HEADS = 4
HG_DK = 128
HG_CHUNK = 64
DF_HEADS = 4
DF_DQK = 64
DF_DV = 128
ROPE_THETA = 500000.0
ROT_DIM = 16
MEM_HEADS = 4
MEM_DH = 128
N_BRANCH = 3
N_GROUPS = 4
EXPERTS_PER_GROUP = 8
N_EXPERTS = 32
TOP_K = 2
DEPTH = 1
DEEPNORM_ALPHA = (2.0 * DEPTH) ** 0.25
LN_EPS = 1e-5
RMS_EPS = 1e-6
LANES = 128
V7X_VMEM_LIMIT = 56 * 1024 * 1024


def _tiles(seq, n_tok):
    return dict(
        proj_rows=min(512, n_tok),
        hgrn_rows=min(512, seq),
        attn_q=min(1024, seq),
        attn_k=min(512, seq),
        attn_rows=128,
        mem_q=min(512, seq),
        merge_rows=min(256, n_tok),
        scatter_rows=min(1024, n_tok),
        moe_rows=256,
        combine_rows=min(256, n_tok),
    )


def _sigmoid(v):
    return 1.0 / (1.0 + jnp.exp(-v))


def _dot(a, b):
    return jnp.dot(a, b, preferred_element_type=F32)


def _dot_nt(a, b):
    return lax.dot_general(a, b, (((1,), (1,)), ((), ())), preferred_element_type=F32)


def _dot_tn(a, b):
    return lax.dot_general(a, b, (((0,), (0,)), ((), ())), preferred_element_type=F32)


def _proj_kernel(x_ref, pos_ref, w_ref, lbraw_ref, invf_ref, p_ref, logf_ref, *, layer):
    tm = x_ref.shape[0]
    wd = logf_ref.shape[1]
    x = x_ref[...].astype(BF16)

    def mm(j):
        return _dot(x, w_ref[:, j * wd:(j + 1) * wd])

    a = lbraw_ref[...]
    e = jnp.exp(a - jnp.max(a, axis=0, keepdims=True))
    sm = e / jnp.sum(e, axis=0, keepdims=True)
    lb = jnp.sum(sm[0:layer + 1, :], axis=0, keepdims=True)

    hq = mm(0)
    p_ref[:, 0:wd] = (hq * _sigmoid(hq)).astype(BF16)
    hf = mm(1)
    forget = lb + (1.0 - lb) * _sigmoid(hf)
    logf_ref[...] = jnp.log(forget)
    p_ref[:, wd:2 * wd] = (1.0 - forget).astype(BF16)
    p_ref[:, 2 * wd:3 * wd] = mm(2).astype(BF16)
    hg = mm(3)
    p_ref[:, 3 * wd:4 * wd] = (hg * _sigmoid(hg)).astype(BF16)

    ang = invf_ref[...] * pos_ref[...].astype(F32)
    c8 = jnp.cos(ang)
    s8 = jnp.sin(ang)
    one = jnp.ones((DF_DQK - ROT_DIM, tm), F32)
    zero = jnp.zeros((DF_DQK - ROT_DIM, tm), F32)
    z8 = jnp.zeros_like(s8)
    cos_t = jnp.concatenate([c8, c8, one, c8, c8, one], axis=0).T
    sin_lo = jnp.concatenate([-s8, z8, zero, -s8, z8, zero], axis=0).T
    sin_hi = jnp.concatenate([z8, s8, zero, z8, s8, zero], axis=0).T
    half = ROT_DIM // 2

    def rope(t):
        return (t * cos_t + pltpu.roll(t, LANES - half, 1) * sin_lo
                + pltpu.roll(t, half, 1) * sin_hi)

    q = mm(4)
    k = mm(5)
    for j in range(wd // LANES):
        sl = slice(j * LANES, (j + 1) * LANES)
        p_ref[:, 4 * wd + j * LANES:4 * wd + (j + 1) * LANES] = (
            rope(q[:, sl]) * (DF_DQK ** -0.5 * math.log2(math.e))).astype(BF16)
        p_ref[:, 5 * wd + j * LANES:5 * wd + (j + 1) * LANES] = rope(k[:, sl]).astype(BF16)
    p_ref[:, 6 * wd:7 * wd] = mm(6).astype(BF16)
    p_ref[:, 7 * wd:8 * wd] = mm(7).astype(BF16)


def _proj(x2, pos3, w_in_b, lbraw, invf8, *, tm, layer):
    n_tok, d = x2.shape
    width = w_in_b.shape[1]
    wd = lbraw.shape[1]
    return pl.pallas_call(
        functools.partial(_proj_kernel, layer=layer),
        out_shape=(jax.ShapeDtypeStruct((n_tok, width), BF16),
                   jax.ShapeDtypeStruct((n_tok, wd), F32)),
        grid=(n_tok // tm,),
        in_specs=[
            pl.BlockSpec((tm, d), lambda i: (i, 0)),
            pl.BlockSpec((None, 1, tm), lambda i: (i, 0, 0)),
            pl.BlockSpec((d, width), lambda i: (0, 0)),
            pl.BlockSpec(lbraw.shape, lambda i: (0, 0)),
            pl.BlockSpec(invf8.shape, lambda i: (0, 0)),
        ],
        out_specs=(pl.BlockSpec((tm, width), lambda i: (i, 0)),
                   pl.BlockSpec((tm, wd), lambda i: (i, 0))),
        compiler_params=pltpu.CompilerParams(
            dimension_semantics=("arbitrary",), vmem_limit_bytes=V7X_VMEM_LIMIT),
        name="proj",
    )(x2, pos3, w_in_b, lbraw, invf8)


def _hgrn_kernel(q_ref, k_ref, v_ref, g_ref, lf_ref, gain_ref, lvl_ref, tri_ref, o_ref, st_ref):
    rows = q_ref.shape[0]
    c = HG_CHUNK
    dk = HG_DK

    @pl.when(pl.program_id(1) == 0)
    def _():
        st_ref[...] = jnp.zeros_like(st_ref)

    lvl = lvl_ref[...]
    tri = tri_ref[...]
    sub = lax.broadcasted_iota(jnp.int32, (c // 8, 8, dk), 1)
    row = lax.broadcasted_iota(jnp.int32, (c, dk), 0)

    def anchors(cum):
        out = []
        for m in (32, 16, 8):
            pieces = []
            for j in range(c // (2 * m)):
                a = j * 2 * m + m - 1
                pieces.append(jnp.broadcast_to(cum[a:a + 1, :], (2 * m, dk)))
            out.append(pieces[0] if len(pieces) == 1 else jnp.concatenate(pieces, axis=0))
        c8 = cum.reshape(c // 8, 8, dk)
        out.append(jnp.broadcast_to(c8[:, 3:4, :], c8.shape).reshape(c, dk))
        a2 = jnp.where(sub < 4, jnp.broadcast_to(c8[:, 1:2, :], c8.shape),
                       jnp.broadcast_to(c8[:, 5:6, :], c8.shape))
        out.append(a2.reshape(c, dk))
        return out

    def chunk_body(ci, carry):
        r0 = pl.multiple_of(ci * c, c)
        for h in range(HG_HEADS):
            hs = slice(h * dk, (h + 1) * dk)
            q = q_ref[pl.ds(r0, c), hs].astype(F32)
            k = k_ref[pl.ds(r0, c), hs].astype(F32)
            v = v_ref[pl.ds(r0, c), hs]
            lf = lf_ref[pl.ds(r0, c), hs]
            lf1 = lf.astype(BF16)
            r1 = lf - lf1.astype(F32)
            lf2 = r1.astype(BF16)
            lf3 = (r1 - lf2.astype(F32)).astype(BF16)
            cum = _dot(tri, lf1) + _dot(tri, lf2) + _dot(tri, lf3)
            e_lvls = [jnp.exp(-jnp.abs(cum - a)) for a in anchors(cum)]
            e_lvls.append(jnp.where((row & 1) == 1, jnp.exp(lf), 1.0))
            scores = jnp.where(lvl == 6, _dot_nt(q.astype(BF16), k.astype(BF16)), 0.0)
            for i, e in enumerate(e_lvls):
                s_i = _dot_nt((q * e).astype(BF16), (k * e).astype(BF16))
                scores = jnp.where(lvl == i, s_i, scores)
            o = _dot(scores.astype(BF16), v)
            ecum = jnp.exp(cum)
            st = st_ref[h]
            o = o + _dot_nt((q * ecum).astype(BF16), st.astype(BF16))
            last = cum[c - 1:c, :]
            k_dec = (k * jnp.exp(last - cum)).astype(BF16)
            st_ref[h] = st * ecum[c - 1:c, :] + _dot_tn(v, k_dec)
            y = o * lax.rsqrt(jnp.mean(o * o, axis=-1, keepdims=True) + RMS_EPS)
            y = y * gain_ref[:, hs] * g_ref[pl.ds(r0, c), hs].astype(F32)
            o_ref[pl.ds(r0, c), hs] = y.astype(o_ref.dtype)
        return carry

    lax.fori_loop(0, rows // c, chunk_body, 0)


def _hgrn_consts():
    c = HG_CHUNK
    t = jnp.arange(c)[:, None]
    s = jnp.arange(c)[None, :]
    x = t ^ s
    lvl = jnp.full((c, c), -1, jnp.int32)
    for i, m in enumerate((32, 16, 8, 4, 2, 1)):
        lvl = jnp.where((t > s) & (x >= m) & (x < 2 * m), i, lvl)
    lvl = jnp.where(t == s, 6, lvl)
    tri = (t >= s).astype(BF16)
    return lvl, tri


def _hgrn(p3, logf3, gain, *, rows):
    b, s, _ = p3.shape
    wd = logf3.shape[2]
    lvl, tri = _hgrn_consts()

    def col(j):
        return pl.BlockSpec((None, rows, wd), lambda bi, si: (bi, si, j))

    return pl.pallas_call(
        _hgrn_kernel,
        out_shape=jax.ShapeDtypeStruct((b, s, wd), BF16),
        grid=(b, s // rows),
        in_specs=[col(0), col(1), col(2), col(3), col(0),
                  pl.BlockSpec(gain.shape, lambda bi, si: (0, 0)),
                  pl.BlockSpec(lvl.shape, lambda bi, si: (0, 0)),
                  pl.BlockSpec(tri.shape, lambda bi, si: (0, 0))],
        out_specs=col(0),
        scratch_shapes=[pltpu.VMEM((HG_HEADS, HG_DK, HG_DK), F32)],
        compiler_params=pltpu.CompilerParams(
            dimension_semantics=("arbitrary", "arbitrary"), vmem_limit_bytes=V7X_VMEM_LIMIT),
        name="hgrn",
    )(p3, p3, p3, p3, logf3, gain, lvl, tri)


def _diffattn_kernel(q_ref, k_ref, v_ref, lq1_ref, lk1_ref, lq2_ref, lk2_ref, gain_ref, o_ref,
                     m_ref, l_ref, acc_ref, *, tk, rq, lam_init):
    tq = q_ref.shape[0]
    qi = pl.program_id(2)
    lane = lax.broadcasted_iota(jnp.int32, (tq, LANES), 1)
    q = q_ref[...]
    zero = jnp.zeros_like(q)
    qm = (jnp.where(lane < DF_DQK, q, zero), jnp.where(lane >= DF_DQK, q, zero))

    m_ref[...] = jnp.full(m_ref.shape, -jnp.inf, F32)
    l_ref[...] = jnp.zeros(l_ref.shape, F32)
    acc_ref[...] = jnp.zeros(acc_ref.shape, F32)

    def tile(k0, diag):
        blocks = [(r, j) for r in range(tq // rq) for j in range(2)]
        n_keys = {r: ((r + 1) * rq if diag else tk) for r in range(tq // rq)}
        scores = {}
        for r, j in blocks:
            rows = slice(r * rq, (r + 1) * rq)
            scores[r, j] = _dot_nt(qm[j][rows], k_ref[pl.ds(k0, n_keys[r]), :])
        probs = {}
        for r, j in blocks:
            rows = slice(r * rq, (r + 1) * rq)
            s = scores[r, j]
            chunks = [s[:, c * LANES:(c + 1) * LANES] for c in range(n_keys[r] // LANES)]
            if diag:
                n_d = rq // LANES
                rr = lax.broadcasted_iota(jnp.int32, (rq, LANES), 0)
                cc = lax.broadcasted_iota(jnp.int32, (rq, LANES), 1)
                for c in range(n_d):
                    idx = len(chunks) - n_d + c
                    chunks[idx] = jnp.where(rr >= cc + c * LANES, chunks[idx], -jnp.inf)
            cmax = functools.reduce(jnp.maximum, chunks)
            m_old = m_ref[j, rows]
            m_new = jnp.maximum(m_old, jnp.max(cmax, axis=-1, keepdims=True))
            alpha = jnp.exp2(m_old - m_new)
            ps = [jnp.exp2(ch - m_new) for ch in chunks]
            l_ref[j, rows] = alpha * l_ref[j, rows] + functools.reduce(lambda a, b: a + b, ps)
            m_ref[j, rows] = m_new
            acc_ref[j, rows] = alpha * acc_ref[j, rows]
            probs[r, j] = jnp.concatenate([x.astype(BF16) for x in ps], axis=1)
        for r, j in blocks:
            rows = slice(r * rq, (r + 1) * rq)
            acc_ref[j, rows] += _dot(probs[r, j], v_ref[pl.ds(k0, n_keys[r]), :])

    def body(ki, carry):
        tile(pl.multiple_of(ki * tk, tk), False)
        return carry

    lax.fori_loop(0, (qi * tq) // tk, body, 0)
    tile(pl.multiple_of(qi * tq, tq), True)

    lam = (jnp.exp(jnp.sum(lq1_ref[...] * lk1_ref[...], keepdims=True))
           - jnp.exp(jnp.sum(lq2_ref[...] * lk2_ref[...], keepdims=True)) + lam_init)
    l0 = jnp.sum(l_ref[0], axis=-1, keepdims=True)
    l1 = jnp.sum(l_ref[1], axis=-1, keepdims=True)
    o = acc_ref[0] / l0 - lam * (acc_ref[1] / l1)
    y = o * lax.rsqrt(jnp.mean(o * o, axis=-1, keepdims=True) + RMS_EPS)
    o_ref[...] = (y * gain_ref[...] * (1.0 - lam_init)).astype(o_ref.dtype)


def _diffattn(p3, lq1, lk1, lq2, lk2, gain, *, tq, tk, rq, lam_init, col0):
    b, s, _ = p3.shape
    assert tq % tk == 0 and tq % rq == 0 and rq % LANES == 0
    small = pl.BlockSpec((1, DF_DQK), lambda bi, hi, qi: (0, 0))
    return pl.pallas_call(
        functools.partial(_diffattn_kernel, tk=tk, rq=rq, lam_init=lam_init),
        out_shape=jax.ShapeDtypeStruct((b, s, DF_HEADS * DF_DV), BF16),
        grid=(b, DF_HEADS, s // tq),
        in_specs=[
            pl.BlockSpec((None, tq, LANES), lambda bi, hi, qi: (bi, qi, col0 + hi)),
            pl.BlockSpec((None, s, LANES), lambda bi, hi, qi: (bi, 0, col0 + DF_HEADS + hi)),
            pl.BlockSpec((None, s, LANES), lambda bi, hi, qi: (bi, 0, col0 + 2 * DF_HEADS + hi)),
            small, small, small, small,
            pl.BlockSpec((1, DF_DV), lambda bi, hi, qi: (0, 0)),
        ],
        out_specs=pl.BlockSpec((None, tq, DF_DV), lambda bi, hi, qi: (bi, qi, hi)),
        scratch_shapes=[pltpu.VMEM((2, tq, LANES), F32), pltpu.VMEM((2, tq, LANES), F32),
                        pltpu.VMEM((2, tq, DF_DV), F32)],
        compiler_params=pltpu.CompilerParams(
            dimension_semantics=("arbitrary", "arbitrary", "arbitrary"),
            vmem_limit_bytes=V7X_VMEM_LIMIT),
        name="diffattn",
    )(p3, p3, p3, lq1, lk1, lq2, lk2, gain)


def _memattn_kernel(q_ref, mem_ref, w_ref, o_ref, kv_ref):
    width = MEM_HEADS * MEM_DH

    @pl.when(pl.program_id(1) == 0)
    def _():
        kv_ref[...] = _dot(mem_ref[...].astype(BF16), w_ref[...]).astype(BF16)

    for h in range(MEM_HEADS):
        hs = slice(h * MEM_DH, (h + 1) * MEM_DH)
        s = _dot_nt(q_ref[:, hs], kv_ref[:, hs]) * (MEM_DH ** -0.5)
        e = jnp.exp(s - jnp.max(s, axis=-1, keepdims=True))
        p = e / jnp.sum(e, axis=-1, keepdims=True)
        o_ref[:, hs] = _dot(p.astype(BF16), kv_ref[:, width + h * MEM_DH:width + (h + 1) * MEM_DH]
                            ).astype(o_ref.dtype)


def _memattn(p3, mem, w_kv_b, *, tq, col0):
    b, s, _ = p3.shape
    n_mem, d = mem.shape[1], mem.shape[2]
    width = MEM_HEADS * MEM_DH
    return pl.pallas_call(
        _memattn_kernel,
        out_shape=jax.ShapeDtypeStruct((b, s, width), BF16),
        grid=(b, s // tq),
        in_specs=[
            pl.BlockSpec((None, tq, width), lambda bi, qi: (bi, qi, col0)),
            pl.BlockSpec((None, n_mem, d), lambda bi, qi: (bi, 0, 0)),
            pl.BlockSpec(w_kv_b.shape, lambda bi, qi: (0, 0)),
        ],
        out_specs=pl.BlockSpec((None, tq, width), lambda bi, qi: (bi, qi, 0)),
        scratch_shapes=[pltpu.VMEM((n_mem, 2 * width), BF16)],
        compiler_params=pltpu.CompilerParams(
            dimension_semantics=("arbitrary", "arbitrary"), vmem_limit_bytes=V7X_VMEM_LIMIT),
        name="memattn",
    )(p3, mem, w_kv_b)


def _layer_norm(z, gain, bias):
    mu = jnp.mean(z, axis=-1, keepdims=True)
    zc = z - mu
    var = jnp.mean(zc * zc, axis=-1, keepdims=True)
    return zc * lax.rsqrt(var + LN_EPS) * gain + bias


def _merge_kernel(x_ref, yh_ref, yd_ref, ym_ref, wg_ref, wbh_ref, wbd_ref, wbm_ref, wo_ref,
                  g1_ref, b1_ref, wr_hi_ref, wr_lo_ref, tri_ref, h_ref, meta_ref, cnt_ref, run_ref):
    tm, d = x_ref.shape

    @pl.when(pl.program_id(0) == 0)
    def _():
        run_ref[...] = jnp.zeros_like(run_ref)

    x = x_ref[...]
    xb = x.astype(BF16)
    merged = None
    for j, (y_ref, wb_ref) in enumerate(((yh_ref, wbh_ref), (yd_ref, wbd_ref), (ym_ref, wbm_ref))):
        gate = _sigmoid(_dot(xb, wg_ref[:, j * d:(j + 1) * d]))
        term = gate * _dot(y_ref[...], wb_ref[...])
        merged = term if merged is None else merged + term
    mix = _dot(merged.astype(BF16), wo_ref[...])
    h = _layer_norm(DEEPNORM_ALPHA * x + mix, g1_ref[...], b1_ref[...])
    h_ref[...] = h

    h_hi = h.astype(BF16)
    h_lo = (h - h_hi.astype(F32)).astype(BF16)
    lg = _dot(h_hi, wr_hi_ref[...]) + _dot(h_hi, wr_lo_ref[...]) + _dot(h_lo, wr_hi_ref[...])

    lane = lax.broadcasted_iota(jnp.int32, (tm, LANES), 1)
    neg = jnp.full_like(lg, -jnp.inf)
    big = jnp.full_like(lane, 4 * LANES)
    is_g = lane < N_GROUPS
    g_max = jnp.max(jnp.where(is_g, lg, neg), axis=-1, keepdims=True)
    g_sum = jnp.sum(jnp.where(is_g, jnp.exp(lg - g_max), 0.0), axis=-1, keepdims=True)
    group_w = 1.0 / g_sum
    g_idx = jnp.min(jnp.where(is_g & (lg == g_max), lane, big), axis=-1, keepdims=True)
    in_grp = ((lane >= N_GROUPS) & (lane < N_GROUPS + N_EXPERTS)
              & (jnp.right_shift(lane - N_GROUPS, 3) == g_idx))
    v1 = jnp.max(jnp.where(in_grp, lg, neg), axis=-1, keepdims=True)
    i1 = jnp.min(jnp.where(in_grp & (lg == v1), lane, big), axis=-1, keepdims=True)
    rest = in_grp & (lane != i1)
    v2 = jnp.max(jnp.where(rest, lg, neg), axis=-1, keepdims=True)
    i2 = jnp.min(jnp.where(rest & (lg == v2), lane, big), axis=-1, keepdims=True)
    e21 = jnp.exp(v2 - v1)
    w1 = group_w / (1.0 + e21)
    w2 = group_w * e21 / (1.0 + e21)

    hot1 = lane == i1
    hot2 = lane == i2
    hot = jnp.where(hot1 | hot2, 1.0, 0.0)
    before = _dot(tri_ref[...], hot.astype(BF16)) + run_ref[...]
    r1 = jnp.sum(jnp.where(hot1, before, 0.0), axis=-1, keepdims=True)
    r2 = jnp.sum(jnp.where(hot2, before, 0.0), axis=-1, keepdims=True)
    run_new = before[tm - 1:tm, :] + hot[tm - 1:tm, :]
    run_ref[...] = run_new
    cnt_ref[...] = run_new

    def put(col, val, acc):
        return jnp.where(lane == col, val, acc)

    meta = jnp.zeros((tm, LANES), F32)
    meta = put(0, (i1 - N_GROUPS).astype(F32), meta)
    meta = put(1, (i2 - N_GROUPS).astype(F32), meta)
    meta = put(2, r1, meta)
    meta = put(3, r2, meta)
    meta = put(4, w1, meta)
    meta = put(5, w2, meta)
    meta_ref[...] = meta


def _merge(x2, yh, yd, ym, wg_b, wbh_b, wbd_b, wbm_b, wo_b, g1, b1, wr_hi, wr_lo, *, tm):
    n_tok, d = x2.shape
    tri = (jnp.arange(tm)[:, None] > jnp.arange(tm)[None, :]).astype(BF16)
    row = lambda w: pl.BlockSpec((tm, w), lambda i: (i, 0))
    full = lambda a: pl.BlockSpec(a.shape, lambda i: (0, 0))
    return pl.pallas_call(
        _merge_kernel,
        out_shape=(jax.ShapeDtypeStruct((n_tok, d), F32),
                   jax.ShapeDtypeStruct((n_tok, LANES), F32),
                   jax.ShapeDtypeStruct((1, LANES), F32)),
        grid=(n_tok // tm,),
        in_specs=[row(d), row(yh.shape[1]), row(yd.shape[1]), row(ym.shape[1]),
                  full(wg_b), full(wbh_b), full(wbd_b), full(wbm_b), full(wo_b),
                  full(g1), full(b1), full(wr_hi), full(wr_lo), full(tri)],
        out_specs=(row(d), row(LANES), pl.BlockSpec((1, LANES), lambda i: (0, 0))),
        scratch_shapes=[pltpu.VMEM((1, LANES), F32)],
        compiler_params=pltpu.CompilerParams(
            dimension_semantics=("arbitrary",), vmem_limit_bytes=V7X_VMEM_LIMIT),
        name="merge",
    )(x2, yh, yd, ym, wg_b, wbh_b, wbd_b, wbm_b, wo_b, g1, b1, wr_hi, wr_lo, tri)


def _scatter_kernel(dest_ref, h_ref, xs_in_hbm, xs_hbm, dest_smem, sem, *, ts):
    del xs_in_hbm
    cp = pltpu.make_async_copy(dest_ref, dest_smem, sem.at[1])
    cp.start()
    cp.wait()

    def issue(t, carry):
        for k in range(TOP_K):
            d = dest_smem[0, 0, TOP_K * t + k]
            pltpu.make_async_copy(h_ref.at[pl.ds(t, 1)], xs_hbm.at[pl.ds(d, 1)], sem.at[0]).start()
        return carry

    lax.fori_loop(0, ts, issue, 0)
    pltpu.make_async_copy(xs_hbm.at[pl.ds(0, TOP_K * ts)], xs_hbm.at[pl.ds(0, TOP_K * ts)],
                          sem.at[0]).wait()


def _scatter(dest3, h, xs_zero, *, ts):
    n_tok, d = h.shape
    return pl.pallas_call(
        functools.partial(_scatter_kernel, ts=ts),
        out_shape=jax.ShapeDtypeStruct(xs_zero.shape, xs_zero.dtype),
        grid=(n_tok // ts,),
        in_specs=[pl.BlockSpec((1, 1, TOP_K * ts), lambda i: (i, 0, 0)),
                  pl.BlockSpec((ts, d), lambda i: (i, 0)),
                  pl.BlockSpec(memory_space=pl.ANY)],
        out_specs=pl.BlockSpec(memory_space=pl.ANY),
        scratch_shapes=[pltpu.SMEM((1, 1, TOP_K * ts), jnp.int32), pltpu.SemaphoreType.DMA((2,))],
        input_output_aliases={2: 0},
        compiler_params=pltpu.CompilerParams(
            dimension_semantics=("arbitrary",), vmem_limit_bytes=V7X_VMEM_LIMIT),
        name="scatter",
    )(dest3, h, xs_zero)


def _experts_kernel(be_ref, nv_ref, x_ref, wg_ref, wu_ref, wd_ref, y_ref):
    j = pl.program_id(0)

    @pl.when(j < nv_ref[0])
    def _():
        x = x_ref[...].astype(BF16)
        g = _dot(x, wg_ref[...])
        u = _dot(x, wu_ref[...])
        act = g * _sigmoid(g) * u
        y_ref[...] = _dot(act.astype(BF16), wd_ref[...])

    @pl.when(j >= nv_ref[0])
    def _():
        y_ref[...] = jnp.zeros_like(y_ref)


def _experts(block_expert, n_valid, xs, wg_b, wu_b, wd_b, *, rows):
    n_slots, d = xs.shape
    de = wg_b.shape[2]
    return pl.pallas_call(
        _experts_kernel,
        out_shape=jax.ShapeDtypeStruct((n_slots, d), F32),
        grid_spec=pltpu.PrefetchScalarGridSpec(
            num_scalar_prefetch=2,
            grid=(n_slots // rows,),
            in_specs=[pl.BlockSpec((rows, d), lambda j, be, nv: (j, 0)),
                      pl.BlockSpec((None, d, de), lambda j, be, nv: (be[j], 0, 0)),
                      pl.BlockSpec((None, d, de), lambda j, be, nv: (be[j], 0, 0)),
                      pl.BlockSpec((None, de, d), lambda j, be, nv: (be[j], 0, 0))],
            out_specs=pl.BlockSpec((rows, d), lambda j, be, nv: (j, 0)),
        ),
        compiler_params=pltpu.CompilerParams(
            dimension_semantics=("arbitrary",), vmem_limit_bytes=V7X_VMEM_LIMIT),
        name="experts",
    )(block_expert, n_valid, xs, wg_b, wu_b, wd_b)


def _combine_kernel(dest_ref, h_ref, meta_ref, g2_ref, b2_ref, y_hbm, o_ref, dest_smem, ybuf, sem,
                    *, tc):
    cp = pltpu.make_async_copy(dest_ref, dest_smem, sem.at[1])
    cp.start()
    cp.wait()

    def issue(t, carry):
        for k in range(TOP_K):
            d = dest_smem[0, 0, TOP_K * t + k]
            pltpu.make_async_copy(y_hbm.at[pl.ds(d, 1)], ybuf.at[k, pl.ds(t, 1)], sem.at[0]).start()
        return carry

    lax.fori_loop(0, tc, issue, 0)
    for k in range(TOP_K):
        pltpu.make_async_copy(y_hbm.at[pl.ds(0, tc)], ybuf.at[k], sem.at[0]).wait()

    meta = meta_ref[...]
    ffn = meta[:, 4:5] * ybuf[0] + meta[:, 5:6] * ybuf[1]
    o_ref[...] = _layer_norm(DEEPNORM_ALPHA * h_ref[...] + ffn, g2_ref[...], b2_ref[...])


def _combine(dest3, h, meta, g2, b2, y, *, tc):
    n_tok, d = h.shape
    return pl.pallas_call(
        functools.partial(_combine_kernel, tc=tc),
        out_shape=jax.ShapeDtypeStruct((n_tok, d), F32),
        grid=(n_tok // tc,),
        in_specs=[pl.BlockSpec((1, 1, TOP_K * tc), lambda i: (i, 0, 0)),
                  pl.BlockSpec((tc, d), lambda i: (i, 0)),
                  pl.BlockSpec((tc, LANES), lambda i: (i, 0)),
                  pl.BlockSpec(g2.shape, lambda i: (0, 0)),
                  pl.BlockSpec(b2.shape, lambda i: (0, 0)),
                  pl.BlockSpec(memory_space=pl.ANY)],
        out_specs=pl.BlockSpec((tc, d), lambda i: (i, 0)),
        scratch_shapes=[pltpu.SMEM((1, 1, TOP_K * tc), jnp.int32),
                        pltpu.VMEM((TOP_K, tc, d), F32),
                        pltpu.SemaphoreType.DMA((2,))],
        compiler_params=pltpu.CompilerParams(
            dimension_semantics=("arbitrary",), vmem_limit_bytes=V7X_VMEM_LIMIT),
        name="combine",
    )(dest3, h, meta, g2, b2, y)


def _w_in_column_order():
    hgw = HG_HEADS * HG_DK
    qkw = DF_HEADS * DF_DQK
    base = 4 * hgw
    cols = list(range(base))
    for first in (base, base + 2 * qkw):
        for h in range(DF_HEADS):
            cols += list(range(first + h * DF_DQK, first + (h + 1) * DF_DQK))
            cols += list(range(first + qkw + h * DF_DQK, first + qkw + (h + 1) * DF_DQK))
    total = base + 4 * qkw + DF_HEADS * DF_DV + MEM_HEADS * MEM_DH
    cols += list(range(base + 4 * qkw, total))
    return jnp.asarray(cols, jnp.int32)


def kernel(x, mem, positions, w_in, w_gates, hgrn_lower_bounds, hgrn_norm_gain, diff_lambda_q1, diff_lambda_k1, diff_lambda_q2, diff_lambda_k2, diff_subln_gain, w_mem_kv, w_branch_hgrn, w_branch_diff, w_branch_mem, w_out, ln1_gain, ln1_bias, w_group_router, w_expert_router, w_expert_gate, w_expert_up, w_expert_down, ln2_gain, ln2_bias):
    b, s, d = x.shape
    n_tok = b * s
    t = _tiles(s, n_tok)
    assert w_in.shape[0] == DEPTH
    layer = 0
    lam_init = 0.8 - 0.6 * math.exp(-0.3 * layer)
    hgw = HG_HEADS * HG_DK

    w_in_b = jnp.take(w_in[layer], _w_in_column_order(), axis=1).astype(BF16)
    inv_freq = ROPE_THETA ** (-jnp.arange(0, ROT_DIM, 2, dtype=F32) / ROT_DIM)
    invf8 = inv_freq.reshape(ROT_DIM // 2, 1)
    w_router = jnp.concatenate(
        [w_group_router[layer], w_expert_router[layer],
         jnp.zeros((d, LANES - N_GROUPS - N_EXPERTS), F32)], axis=1)
    wr_hi = w_router.astype(BF16)
    wr_lo = (w_router - wr_hi.astype(F32)).astype(BF16)

    x2 = x.reshape(n_tok, d)
    pos3 = positions.reshape(n_tok // t["proj_rows"], 1, t["proj_rows"])
    p, logf = _proj(x2, pos3, w_in_b, hgrn_lower_bounds, invf8, tm=t["proj_rows"], layer=layer)
    p3 = p.reshape(b, s, p.shape[1])

    y_hg = _hgrn(p3, logf.reshape(b, s, hgw), hgrn_norm_gain[layer].reshape(1, hgw),
                 rows=t["hgrn_rows"])
    y_df = _diffattn(p3, diff_lambda_q1[layer].reshape(1, -1), diff_lambda_k1[layer].reshape(1, -1),
                     diff_lambda_q2[layer].reshape(1, -1), diff_lambda_k2[layer].reshape(1, -1),
                     diff_subln_gain[layer].reshape(1, -1), tq=t["attn_q"], tk=t["attn_k"], rq=t["attn_rows"],
                     lam_init=lam_init, col0=4 * hgw // LANES)
    y_mem = _memattn(p3, mem, w_mem_kv[layer].astype(BF16), tq=t["mem_q"],
                     col0=7 * hgw // (MEM_HEADS * MEM_DH))

    h, meta, counts_f = _merge(
        x2, y_hg.reshape(n_tok, -1), y_df.reshape(n_tok, -1), y_mem.reshape(n_tok, -1),
        w_gates[layer].astype(BF16), w_branch_hgrn[layer].astype(BF16),
        w_branch_diff[layer].astype(BF16), w_branch_mem[layer].astype(BF16),
        w_out[layer].astype(BF16), ln1_gain[layer].reshape(1, d), ln1_bias[layer].reshape(1, d),
        wr_hi, wr_lo, tm=t["merge_rows"])

    rows = t["moe_rows"]
    counts = counts_f[0, N_GROUPS:N_GROUPS + N_EXPERTS].astype(jnp.int32)
    padded = (counts + rows - 1) // rows * rows
    padded_end = jnp.cumsum(padded)
    padded_start = padded_end - padded
    expert_id = meta[:, 0:TOP_K].astype(jnp.int32)
    rank = meta[:, TOP_K:2 * TOP_K].astype(jnp.int32)
    dest = padded_start[expert_id] + rank
    n_slots = n_tok * TOP_K + N_EXPERTS * rows
    n_blocks = n_slots // rows
    block_start = jnp.arange(n_blocks, dtype=jnp.int32) * rows
    block_expert = jnp.minimum(jnp.searchsorted(padded_end, block_start, side="right"),
                               N_EXPERTS - 1).astype(jnp.int32)
    n_valid = (padded_end[-1:] // rows).astype(jnp.int32)

    ts = t["scatter_rows"]
    xs = _scatter(dest.reshape(n_tok // ts, 1, TOP_K * ts), h, jnp.zeros((n_slots, d), F32), ts=ts)
    y = _experts(block_expert, n_valid, xs, w_expert_gate[layer].astype(BF16),
                 w_expert_up[layer].astype(BF16), w_expert_down[layer].astype(BF16), rows=rows)
    tc = t["combine_rows"]
    out = _combine(dest.reshape(n_tok // tc, 1, TOP_K * tc), h, meta,
                   ln2_gain[layer].reshape(1, d), ln2_bias[layer].reshape(1, d), y, tc=tc)
    return out.reshape(b, s, d)
```

```python
import functools
import math

import jax
import jax.numpy as jnp
from jax import lax
from jax.experimental import pallas as pl
from jax.experimental.pallas import tpu as pltpu

F32 = jnp.float32
BF16 = jnp.bfloat16

HG_HEADS = 4
HG_DK = 128
HG_CHUNK = 64
DF_HEADS = 4
DF_DQK = 64
DF_DV = 128
ROPE_THETA = 500000.0
ROT_DIM = 16
MEM_HEADS = 4
MEM_DH = 128
N_BRANCH = 3
N_GROUPS = 4
EXPERTS_PER_GROUP = 8
N_EXPERTS = 32
TOP_K = 2
DEPTH = 1
DEEPNORM_ALPHA = (2.0 * DEPTH) ** 0.25
LN_EPS = 1e-5
RMS_EPS = 1e-6
LANES = 128
V7X_VMEM_LIMIT = 56 * 1024 * 1024


def _tiles(seq, n_tok):
    return dict(
        proj_rows=min(512, n_tok),
        hgrn_rows=min(512, seq),
        hgrn_par=4,
        attn_q=min(1024, seq),
        attn_k=min(512, seq),
        attn_rows=128,
        mem_q=min(512, seq),
        merge_rows=min(1024, n_tok),
        merge_parts=4,
        scatter_rows=min(1024, n_tok),
        moe_rows=512,
        combine_rows=min(256, n_tok),
    )


def _sigmoid(v):
    return 1.0 / (1.0 + jnp.exp(-v))


def _dot(a, b):
    return jnp.dot(a, b, preferred_element_type=F32)


def _dot_nt(a, b):
    return lax.dot_general(a, b, (((1,), (1,)), ((), ())), preferred_element_type=F32)


def _dot_tn(a, b):
    return lax.dot_general(a, b, (((0,), (0,)), ((), ())), preferred_element_type=F32)


def _proj_kernel(x_ref, pos_ref, w_ref, lbraw_ref, invf_ref, p_ref, logf_ref, *, layer):
    tm = x_ref.shape[0]
    wd = logf_ref.shape[1]
    x = x_ref[...].astype(BF16)

    def mm(j):
        return _dot(x, w_ref[:, j * wd:(j + 1) * wd])

    a = lbraw_ref[...]
    e = jnp.exp(a - jnp.max(a, axis=0, keepdims=True))
    sm = e / jnp.sum(e, axis=0, keepdims=True)
    lb = jnp.sum(sm[0:layer + 1, :], axis=0, keepdims=True)

    hq = mm(0)
    p_ref[:, 0:wd] = (hq * _sigmoid(hq)).astype(BF16)
    hf = mm(1)
    forget = lb + (1.0 - lb) * _sigmoid(hf)
    logf_ref[...] = jnp.log(forget)
    p_ref[:, wd:2 * wd] = (1.0 - forget).astype(BF16)
    p_ref[:, 2 * wd:3 * wd] = mm(2).astype(BF16)
    hg = mm(3)
    p_ref[:, 3 * wd:4 * wd] = (hg * _sigmoid(hg)).astype(BF16)

    ang = invf_ref[...] * pos_ref[...].astype(F32)
    c8 = jnp.cos(ang)
    s8 = jnp.sin(ang)
    one = jnp.ones((DF_DQK - ROT_DIM, tm), F32)
    zero = jnp.zeros((DF_DQK - ROT_DIM, tm), F32)
    z8 = jnp.zeros_like(s8)
    cos_t = jnp.concatenate([c8, c8, one, c8, c8, one], axis=0).T
    sin_lo = jnp.concatenate([-s8, z8, zero, -s8, z8, zero], axis=0).T
    sin_hi = jnp.concatenate([z8, s8, zero, z8, s8, zero], axis=0).T
    half = ROT_DIM // 2

    def rope(t):
        return (t * cos_t + pltpu.roll(t, LANES - half, 1) * sin_lo
                + pltpu.roll(t, half, 1) * sin_hi)

    q = mm(4)
    k = mm(5)
    for j in range(wd // LANES):
        sl = slice(j * LANES, (j + 1) * LANES)
        p_ref[:, 4 * wd + j * LANES:4 * wd + (j + 1) * LANES] = (
            rope(q[:, sl]) * (DF_DQK ** -0.5 * math.log2(math.e))).astype(BF16)
        p_ref[:, 5 * wd + j * LANES:5 * wd + (j + 1) * LANES] = rope(k[:, sl]).astype(BF16)
    p_ref[:, 6 * wd:7 * wd] = mm(6).astype(BF16)
    p_ref[:, 7 * wd:8 * wd] = mm(7).astype(BF16)


def _proj(x2, pos3, w_in_b, lbraw, invf8, *, tm, layer):
    n_tok, d = x2.shape
    width = w_in_b.shape[1]
    wd = lbraw.shape[1]
    return pl.pallas_call(
        functools.partial(_proj_kernel, layer=layer),
        out_shape=(jax.ShapeDtypeStruct((n_tok, width), BF16),
                   jax.ShapeDtypeStruct((n_tok, wd), F32)),
        grid=(n_tok // tm,),
        in_specs=[
            pl.BlockSpec((tm, d), lambda i: (i, 0)),
            pl.BlockSpec((None, 1, tm), lambda i: (i, 0, 0)),
            pl.BlockSpec((d, width), lambda i: (0, 0)),
            pl.BlockSpec(lbraw.shape, lambda i: (0, 0)),
            pl.BlockSpec(invf8.shape, lambda i: (0, 0)),
        ],
        out_specs=(pl.BlockSpec((tm, width), lambda i: (i, 0)),
                   pl.BlockSpec((tm, wd), lambda i: (i, 0))),
        compiler_params=pltpu.CompilerParams(
            dimension_semantics=("arbitrary",), vmem_limit_bytes=V7X_VMEM_LIMIT),
        name="proj",
    )(x2, pos3, w_in_b, lbraw, invf8)


def _hgrn_kernel(q_ref, k_ref, v_ref, g_ref, lf_ref, gain_ref, lvl_ref, tri_ref, o_ref, st_ref, *,
                 par):
    rows = q_ref.shape[0]
    c = HG_CHUNK
    dk = HG_DK

    @pl.when(pl.program_id(1) == 0)
    def _():
        st_ref[...] = jnp.zeros_like(st_ref)

    lvl = lvl_ref[...]
    tri = tri_ref[...]
    sub = lax.broadcasted_iota(jnp.int32, (c // 8, 8, dk), 1)
    row = lax.broadcasted_iota(jnp.int32, (c, dk), 0)

    def anchors(cum):
        out = []
        for m in (32, 16, 8):
            pieces = []
            for j in range(c // (2 * m)):
                a = j * 2 * m + m - 1
                pieces.append(jnp.broadcast_to(cum[a:a + 1, :], (2 * m, dk)))
            out.append(pieces[0] if len(pieces) == 1 else jnp.concatenate(pieces, axis=0))
        c8 = cum.reshape(c // 8, 8, dk)
        out.append(jnp.broadcast_to(c8[:, 3:4, :], c8.shape).reshape(c, dk))
        a2 = jnp.where(sub < 4, jnp.broadcast_to(c8[:, 1:2, :], c8.shape),
                       jnp.broadcast_to(c8[:, 5:6, :], c8.shape))
        out.append(a2.reshape(c, dk))
        return out

    def chunk_body(ci, carry):
        units = [(cc, h) for cc in range(par) for h in range(HG_HEADS)]
        r0 = [pl.multiple_of((ci * par + cc) * c, c) for cc in range(par)]
        hsl = [slice(h * dk, (h + 1) * dk) for h in range(HG_HEADS)]
        q = {u: q_ref[pl.ds(r0[u[0]], c), hsl[u[1]]].astype(F32) for u in units}
        k = {u: k_ref[pl.ds(r0[u[0]], c), hsl[u[1]]].astype(F32) for u in units}
        v = {u: v_ref[pl.ds(r0[u[0]], c), hsl[u[1]]] for u in units}
        lf = {u: lf_ref[pl.ds(r0[u[0]], c), hsl[u[1]]] for u in units}
        cum = {}
        for u in units:
            lf1 = lf[u].astype(BF16)
            r1 = lf[u] - lf1.astype(F32)
            lf2 = r1.astype(BF16)
            lf3 = (r1 - lf2.astype(F32)).astype(BF16)
            cum[u] = _dot(tri, lf1) + _dot(tri, lf2) + _dot(tri, lf3)
        operands = {}
        for u in units:
            e_lvls = [jnp.exp(-jnp.abs(cum[u] - a)) for a in anchors(cum[u])]
            e_lvls.append(jnp.where((row & 1) == 1, jnp.exp(lf[u]), 1.0))
            ops = [((q[u] * e).astype(BF16), (k[u] * e).astype(BF16)) for e in e_lvls]
            ops.append((q[u].astype(BF16), k[u].astype(BF16)))
            operands[u] = ops
        ecum = {u: jnp.exp(cum[u]) for u in units}
        k_dec = {u: (k[u] * jnp.exp(cum[u][c - 1:c, :] - cum[u])).astype(BF16) for u in units}
        q_dec = {u: (q[u] * ecum[u]).astype(BF16) for u in units}
        level_dots = {u: [_dot_nt(qe, ke) for qe, ke in operands[u]] for u in units}
        st = [st_ref[h] for h in range(HG_HEADS)]
        o_inter = {}
        for u in units:
            h = u[1]
            o_inter[u] = _dot_nt(q_dec[u], st[h].astype(BF16))
            st[h] = st[h] * ecum[u][c - 1:c, :] + _dot_tn(v[u], k_dec[u])
        for h in range(HG_HEADS):
            st_ref[h] = st[h]
        scores = {}
        for u in units:
            sc = jnp.zeros((c, c), F32)
            for i, d in enumerate(level_dots[u]):
                sc = jnp.where(lvl == i, d, sc)
            scores[u] = sc.astype(BF16)
        for u in units:
            o = o_inter[u] + _dot(scores[u], v[u])
            y = o * lax.rsqrt(jnp.mean(o * o, axis=-1, keepdims=True) + RMS_EPS)
            y = y * gain_ref[:, hsl[u[1]]] * g_ref[pl.ds(r0[u[0]], c), hsl[u[1]]].astype(F32)
            o_ref[pl.ds(r0[u[0]], c), hsl[u[1]]] = y.astype(o_ref.dtype)
        return carry

    lax.fori_loop(0, rows // (c * par), chunk_body, 0)


def _hgrn_consts():
    c = HG_CHUNK
    t = jnp.arange(c)[:, None]
    s = jnp.arange(c)[None, :]
    x = t ^ s
    lvl = jnp.full((c, c), -1, jnp.int32)
    for i, m in enumerate((32, 16, 8, 4, 2, 1)):
        lvl = jnp.where((t > s) & (x >= m) & (x < 2 * m), i, lvl)
    lvl = jnp.where(t == s, 6, lvl)
    tri = (t >= s).astype(BF16)
    return lvl, tri


def _hgrn(p3, logf3, gain, *, rows, par):
    b, s, _ = p3.shape
    wd = logf3.shape[2]
    lvl, tri = _hgrn_consts()

    def col(j):
        return pl.BlockSpec((None, rows, wd), lambda bi, si: (bi, si, j))

    return pl.pallas_call(
        functools.partial(_hgrn_kernel, par=par),
        out_shape=jax.ShapeDtypeStruct((b, s, wd), BF16),
        grid=(b, s // rows),
        in_specs=[col(0), col(1), col(2), col(3), col(0),
                  pl.BlockSpec(gain.shape, lambda bi, si: (0, 0)),
                  pl.BlockSpec(lvl.shape, lambda bi, si: (0, 0)),
                  pl.BlockSpec(tri.shape, lambda bi, si: (0, 0))],
        out_specs=col(0),
        scratch_shapes=[pltpu.VMEM((HG_HEADS, HG_DK, HG_DK), F32)],
        compiler_params=pltpu.CompilerParams(
            dimension_semantics=("arbitrary", "arbitrary"), vmem_limit_bytes=V7X_VMEM_LIMIT),
        name="hgrn",
    )(p3, p3, p3, p3, logf3, gain, lvl, tri)


def _diffattn_kernel(q_ref, k_ref, v_ref, lq1_ref, lk1_ref, lq2_ref, lk2_ref, gain_ref, o_ref,
                     m_ref, l_ref, acc_ref, *, tk, rq, lam_init):
    tq = q_ref.shape[0]
    qi = pl.program_id(2)
    lane = lax.broadcasted_iota(jnp.int32, (tq, LANES), 1)
    q = q_ref[...]
    zero = jnp.zeros_like(q)
    qm = (jnp.where(lane < DF_DQK, q, zero), jnp.where(lane >= DF_DQK, q, zero))

    m_ref[...] = jnp.full(m_ref.shape, -jnp.inf, F32)
    l_ref[...] = jnp.zeros(l_ref.shape, F32)
    acc_ref[...] = jnp.zeros(acc_ref.shape, F32)

    def tile(k0, diag):
        blocks = [(r, j) for r in range(tq // rq) for j in range(2)]
        n_keys = {r: ((r + 1) * rq if diag else tk) for r in range(tq // rq)}
        scores = {}
        for r, j in blocks:
            rows = slice(r * rq, (r + 1) * rq)
            scores[r, j] = _dot_nt(qm[j][rows], k_ref[pl.ds(k0, n_keys[r]), :])
        probs = {}
        for r, j in blocks:
            rows = slice(r * rq, (r + 1) * rq)
            s = scores[r, j]
            chunks = [s[:, c * LANES:(c + 1) * LANES] for c in range(n_keys[r] // LANES)]
            if diag:
                n_d = rq // LANES
                rr = lax.broadcasted_iota(jnp.int32, (rq, LANES), 0)
                cc = lax.broadcasted_iota(jnp.int32, (rq, LANES), 1)
                for c in range(n_d):
                    idx = len(chunks) - n_d + c
                    chunks[idx] = jnp.where(rr >= cc + c * LANES, chunks[idx], -jnp.inf)
            cmax = functools.reduce(jnp.maximum, chunks)
            m_old = m_ref[j, rows]
            m_new = jnp.maximum(m_old, jnp.max(cmax, axis=-1, keepdims=True))
            alpha = jnp.exp2(m_old - m_new)
            ps = [jnp.exp2(ch - m_new) for ch in chunks]
            l_ref[j, rows] = alpha * l_ref[j, rows] + functools.reduce(lambda a, b: a + b, ps)
            m_ref[j, rows] = m_new
            acc_ref[j, rows] = alpha * acc_ref[j, rows]
            probs[r, j] = jnp.concatenate([x.astype(BF16) for x in ps], axis=1)
        for r, j in blocks:
            rows = slice(r * rq, (r + 1) * rq)
            acc_ref[j, rows] += _dot(probs[r, j], v_ref[pl.ds(k0, n_keys[r]), :])

    def body(ki, carry):
        tile(pl.multiple_of(ki * tk, tk), False)
        return carry

    lax.fori_loop(0, (qi * tq) // tk, body, 0)
    tile(pl.multiple_of(qi * tq, tq), True)

    lam = (jnp.exp(jnp.sum(lq1_ref[...] * lk1_ref[...], keepdims=True))
           - jnp.exp(jnp.sum(lq2_ref[...] * lk2_ref[...], keepdims=True)) + lam_init)
    l0 = jnp.sum(l_ref[0], axis=-1, keepdims=True)
    l1 = jnp.sum(l_ref[1], axis=-1, keepdims=True)
    o = acc_ref[0] / l0 - lam * (acc_ref[1] / l1)
    y = o * lax.rsqrt(jnp.mean(o * o, axis=-1, keepdims=True) + RMS_EPS)
    o_ref[...] = (y * gain_ref[...] * (1.0 - lam_init)).astype(o_ref.dtype)


def _diffattn(p3, lq1, lk1, lq2, lk2, gain, *, tq, tk, rq, lam_init, col0):
    b, s, _ = p3.shape
    assert tq % tk == 0 and tq % rq == 0 and rq % LANES == 0
    small = pl.BlockSpec((1, DF_DQK), lambda bi, hi, qi: (0, 0))
    return pl.pallas_call(
        functools.partial(_diffattn_kernel, tk=tk, rq=rq, lam_init=lam_init),
        out_shape=jax.ShapeDtypeStruct((b, s, DF_HEADS * DF_DV), BF16),
        grid=(b, DF_HEADS, s // tq),
        in_specs=[
            pl.BlockSpec((None, tq, LANES), lambda bi, hi, qi: (bi, qi, col0 + hi)),
            pl.BlockSpec((None, s, LANES), lambda bi, hi, qi: (bi, 0, col0 + DF_HEADS + hi)),
            pl.BlockSpec((None, s, LANES), lambda bi, hi, qi: (bi, 0, col0 + 2 * DF_HEADS + hi)),
            small, small, small, small,
            pl.BlockSpec((1, DF_DV), lambda bi, hi, qi: (0, 0)),
        ],
        out_specs=pl.BlockSpec((None, tq, DF_DV), lambda bi, hi, qi: (bi, qi, hi)),
        scratch_shapes=[pltpu.VMEM((2, tq, LANES), F32), pltpu.VMEM((2, tq, LANES), F32),
                        pltpu.VMEM((2, tq, DF_DV), F32)],
        compiler_params=pltpu.CompilerParams(
            dimension_semantics=("arbitrary", "arbitrary", "arbitrary"),
            vmem_limit_bytes=V7X_VMEM_LIMIT),
        name="diffattn",
    )(p3, p3, p3, lq1, lk1, lq2, lk2, gain)


def _memattn_kernel(q_ref, mem_ref, w_ref, o_ref, kv_ref):
    width = MEM_HEADS * MEM_DH

    @pl.when(pl.program_id(1) == 0)
    def _():
        kv_ref[...] = _dot(mem_ref[...].astype(BF16), w_ref[...]).astype(BF16)

    for h in range(MEM_HEADS):
        hs = slice(h * MEM_DH, (h + 1) * MEM_DH)
        s = _dot_nt(q_ref[:, hs], kv_ref[:, hs]) * (MEM_DH ** -0.5)
        e = jnp.exp(s - jnp.max(s, axis=-1, keepdims=True))
        p = e / jnp.sum(e, axis=-1, keepdims=True)
        o_ref[:, hs] = _dot(p.astype(BF16), kv_ref[:, width + h * MEM_DH:width + (h + 1) * MEM_DH]
                            ).astype(o_ref.dtype)


def _memattn(p3, mem, w_kv_b, *, tq, col0):
    b, s, _ = p3.shape
    n_mem, d = mem.shape[1], mem.shape[2]
    width = MEM_HEADS * MEM_DH
    return pl.pallas_call(
        _memattn_kernel,
        out_shape=jax.ShapeDtypeStruct((b, s, width), BF16),
        grid=(b, s // tq),
        in_specs=[
            pl.BlockSpec((None, tq, width), lambda bi, qi: (bi, qi, col0)),
            pl.BlockSpec((None, n_mem, d), lambda bi, qi: (bi, 0, 0)),
            pl.BlockSpec(w_kv_b.shape, lambda bi, qi: (0, 0)),
        ],
        out_specs=pl.BlockSpec((None, tq, width), lambda bi, qi: (bi, qi, 0)),
        scratch_shapes=[pltpu.VMEM((n_mem, 2 * width), BF16)],
        compiler_params=pltpu.CompilerParams(
            dimension_semantics=("arbitrary", "arbitrary"), vmem_limit_bytes=V7X_VMEM_LIMIT),
        name="memattn",
    )(p3, mem, w_kv_b)


def _layer_norm(z, gain, bias):
    mu = jnp.mean(z, axis=-1, keepdims=True)
    zc = z - mu
    var = jnp.mean(zc * zc, axis=-1, keepdims=True)
    return zc * lax.rsqrt(var + LN_EPS) * gain + bias


def _merge_kernel(x_ref, yh_ref, yd_ref, ym_ref, wg_ref, wbh_ref, wbd_ref, wbm_ref, wo_ref,
                  g1_ref, b1_ref, wr_cat_ref, tri_ref, h_ref, meta_ref, cnt_ref, run_ref,
                  *, parts):
    tm, d = x_ref.shape
    rp = tm // parts

    @pl.when(pl.program_id(0) == 0)
    def _():
        run_ref[...] = jnp.zeros_like(run_ref)

    lane = lax.broadcasted_iota(jnp.int32, (rp, LANES), 1)
    neg = jnp.full((rp, LANES), -jnp.inf, F32)
    big = jnp.full((rp, LANES), 4 * LANES, jnp.int32)
    is_g = lane < N_GROUPS
    branches = ((yh_ref, wbh_ref), (yd_ref, wbd_ref), (ym_ref, wbm_ref))

    def project(p):
        rows = slice(p * rp, (p + 1) * rp)
        x = x_ref[rows, :]
        xb = x.astype(BF16)
        merged = None
        for j, (y_ref, wb_ref) in enumerate(branches):
            gate = _sigmoid(_dot(xb, wg_ref[:, j * d:(j + 1) * d]))
            term = gate * _dot(y_ref[rows, :], wb_ref[...])
            merged = term if merged is None else merged + term
        return x, merged.astype(BF16)

    def norm(p, x, mix):
        h = _layer_norm(DEEPNORM_ALPHA * x + mix, g1_ref[...], b1_ref[...])
        h_ref[p * rp:(p + 1) * rp, :] = h
        h_hi = h.astype(BF16)
        h_lo = (h - h_hi.astype(F32)).astype(BF16)
        return h_hi, h_lo

    def logits(h_hi, h_lo):
        t = _dot(h_hi, wr_cat_ref[...])
        return t[:, :LANES] + t[:, LANES:] + _dot(h_lo, wr_cat_ref[:, :LANES])

    def route(lg):
        g_max = jnp.max(jnp.where(is_g, lg, neg), axis=-1, keepdims=True)
        g_sum = jnp.sum(jnp.where(is_g, jnp.exp(lg - g_max), 0.0), axis=-1, keepdims=True)
        group_w = 1.0 / g_sum
        g_idx = jnp.min(jnp.where(is_g & (lg == g_max), lane, big), axis=-1, keepdims=True)
        in_grp = ((lane >= N_GROUPS) & (lane < N_GROUPS + N_EXPERTS)
                  & (jnp.right_shift(lane - N_GROUPS, 3) == g_idx))
        v1 = jnp.max(jnp.where(in_grp, lg, neg), axis=-1, keepdims=True)
        i1 = jnp.min(jnp.where(in_grp & (lg == v1), lane, big), axis=-1, keepdims=True)
        rest = in_grp & (lane != i1)
        v2 = jnp.max(jnp.where(rest, lg, neg), axis=-1, keepdims=True)
        i2 = jnp.min(jnp.where(rest & (lg == v2), lane, big), axis=-1, keepdims=True)
        e21 = jnp.exp(v2 - v1)
        w1 = group_w / (1.0 + e21)
        w2 = group_w * e21 / (1.0 + e21)
        return i1, i2, w1, w2

    def finish(p, routed, run):
        i1, i2, w1, w2 = routed
        hot1 = lane == i1
        hot2 = lane == i2
        hot = jnp.where(hot1 | hot2, 1.0, 0.0)
        before = _dot(tri_ref[...], hot.astype(BF16)) + run
        r1 = jnp.sum(jnp.where(hot1, before, 0.0), axis=-1, keepdims=True)
        r2 = jnp.sum(jnp.where(hot2, before, 0.0), axis=-1, keepdims=True)
        meta = jnp.zeros((rp, LANES), F32)
        for col, val in enumerate(((i1 - N_GROUPS).astype(F32), (i2 - N_GROUPS).astype(F32),
                                   r1, r2, w1, w2)):
            meta = jnp.where(lane == col, val, meta)
        meta_ref[p * rp:(p + 1) * rp, :] = meta
        return before[rp - 1:rp, :] + hot[rp - 1:rp, :]

    run = run_ref[...]
    state = {}
    for step in range(parts + 4):
        p = step - 4
        if 0 <= p < parts:
            run = finish(p, state.pop(("routed", p)), run)
        p = step - 3
        if 0 <= p < parts:
            state["routed", p] = route(state.pop(("lg", p)))
        p = step - 2
        if 0 <= p < parts:
            state["lg", p] = logits(*state.pop(("h", p)))
        p = step - 1
        if 0 <= p < parts:
            x, merged = state.pop(("proj", p))
            state["h", p] = norm(p, x, _dot(merged, wo_ref[...]))
        p = step
        if 0 <= p < parts:
            state["proj", p] = project(p)
    run_ref[...] = run
    cnt_ref[...] = run


def _merge(x2, yh, yd, ym, wg_b, wbh_b, wbd_b, wbm_b, wo_b, g1, b1, wr_cat, *, tm, parts):
    n_tok, d = x2.shape
    rp = tm // parts
    tri = (jnp.arange(rp)[:, None] > jnp.arange(rp)[None, :]).astype(BF16)
    row = lambda w: pl.BlockSpec((tm, w), lambda i: (i, 0))
    full = lambda a: pl.BlockSpec(a.shape, lambda i: (0, 0))
    return pl.pallas_call(
        functools.partial(_merge_kernel, parts=parts),
        out_shape=(jax.ShapeDtypeStruct((n_tok, d), F32),
                   jax.ShapeDtypeStruct((n_tok, LANES), F32),
                   jax.ShapeDtypeStruct((1, LANES), F32)),
        grid=(n_tok // tm,),
        in_specs=[row(d), row(yh.shape[1]), row(yd.shape[1]), row(ym.shape[1]),
                  full(wg_b), full(wbh_b), full(wbd_b), full(wbm_b), full(wo_b),
                  full(g1), full(b1), full(wr_cat), full(tri)],
        out_specs=(row(d), row(LANES), pl.BlockSpec((1, LANES), lambda i: (0, 0))),
        scratch_shapes=[pltpu.VMEM((1, LANES), F32)],
        compiler_params=pltpu.CompilerParams(
            dimension_semantics=("arbitrary",), vmem_limit_bytes=V7X_VMEM_LIMIT),
        name="merge",
    )(x2, yh, yd, ym, wg_b, wbh_b, wbd_b, wbm_b, wo_b, g1, b1, wr_cat, tri)


def _scatter_kernel(pend_ref, psize_ref, nv_ref, dest_ref, h_ref, xs_hbm, dest_smem, zero_ref,
                    row_sem, aux_sem, *, ts, rows):
    cp = pltpu.make_async_copy(dest_ref, dest_smem, aux_sem)
    cp.start()
    cp.wait()
    n_blocks = xs_hbm.shape[0] // rows

    @pl.when(pl.program_id(0) == 0)
    def _():
        zero_ref[...] = jnp.zeros_like(zero_ref)

        def fill(start):
            return pltpu.make_async_copy(zero_ref, xs_hbm.at[pl.ds(start, rows)], aux_sem)

        def fills(act):
            for e in range(N_EXPERTS):
                @pl.when(psize_ref[e] > 0)
                def _():
                    act(fill(pl.multiple_of(pend_ref[e] - rows, rows)))
            for blk in range(n_blocks - N_EXPERTS, n_blocks):
                @pl.when(blk >= nv_ref[0])
                def _():
                    act(fill(blk * rows))

        fills(lambda c: c.start())
        fills(lambda c: c.wait())

    def issue(t, carry):
        for k in range(TOP_K):
            d = dest_smem[0, 0, TOP_K * t + k]
            pltpu.make_async_copy(h_ref.at[pl.ds(t, 1)], xs_hbm.at[pl.ds(d, 1)], row_sem).start()
        return carry

    lax.fori_loop(0, ts, issue, 0, unroll=8)
    pltpu.make_async_copy(xs_hbm.at[pl.ds(0, TOP_K * ts)], xs_hbm.at[pl.ds(0, TOP_K * ts)],
                          row_sem).wait()


def _scatter(padded_end, padded, n_valid, dest3, h, *, ts, rows, n_slots):
    n_tok, d = h.shape
    return pl.pallas_call(
        functools.partial(_scatter_kernel, ts=ts, rows=rows),
        out_shape=jax.ShapeDtypeStruct((n_slots, d), h.dtype),
        grid_spec=pltpu.PrefetchScalarGridSpec(
            num_scalar_prefetch=3,
            grid=(n_tok // ts,),
            in_specs=[pl.BlockSpec((1, 1, TOP_K * ts), lambda i, pe, ps, nv: (i, 0, 0)),
                      pl.BlockSpec((ts, d), lambda i, pe, ps, nv: (i, 0))],
            out_specs=pl.BlockSpec(memory_space=pl.ANY),
            scratch_shapes=[pltpu.SMEM((1, 1, TOP_K * ts), jnp.int32),
                            pltpu.VMEM((rows, d), h.dtype),
                            pltpu.SemaphoreType.DMA, pltpu.SemaphoreType.DMA],
        ),
        compiler_params=pltpu.CompilerParams(
            dimension_semantics=("arbitrary",), vmem_limit_bytes=V7X_VMEM_LIMIT),
        name="scatter",
    )(padded_end, padded, n_valid, dest3, h)


def _experts_kernel(be_ref, nv_ref, x_ref, wg_ref, wu_ref, wd_ref, y_ref):
    j = pl.program_id(0)

    @pl.when(j < nv_ref[0])
    def _():
        half = x_ref.shape[0] // 2
        acts = []
        for p in range(2):
            x = x_ref[p * half:(p + 1) * half, :].astype(BF16)
            g = _dot(x, wg_ref[...])
            u = _dot(x, wu_ref[...])
            acts.append((g * _sigmoid(g) * u).astype(BF16))
        for p in range(2):
            y_ref[p * half:(p + 1) * half, :] = _dot(acts[p], wd_ref[...])

    @pl.when(j >= nv_ref[0])
    def _():
        y_ref[...] = jnp.zeros_like(y_ref)


def _experts(block_expert, n_valid, xs, wg_b, wu_b, wd_b, *, rows):
    n_slots, d = xs.shape
    de = wg_b.shape[2]
    return pl.pallas_call(
        _experts_kernel,
        out_shape=jax.ShapeDtypeStruct((n_slots, d), F32),
        grid_spec=pltpu.PrefetchScalarGridSpec(
            num_scalar_prefetch=2,
            grid=(n_slots // rows,),
            in_specs=[pl.BlockSpec((rows, d), lambda j, be, nv: (jnp.minimum(j, nv[0] - 1), 0)),
                      pl.BlockSpec((None, d, de), lambda j, be, nv: (be[j], 0, 0)),
                      pl.BlockSpec((None, d, de), lambda j, be, nv: (be[j], 0, 0)),
                      pl.BlockSpec((None, de, d), lambda j, be, nv: (be[j], 0, 0))],
            out_specs=pl.BlockSpec((rows, d), lambda j, be, nv: (j, 0)),
        ),
        compiler_params=pltpu.CompilerParams(
            dimension_semantics=("arbitrary",), vmem_limit_bytes=V7X_VMEM_LIMIT),
        name="experts",
    )(block_expert, n_valid, xs, wg_b, wu_b, wd_b)


def _combine_kernel(dest_ref, dest_next_ref, h_ref, meta_ref, g2_ref, b2_ref, y_hbm, o_ref,
                    dest_smem, ybuf, row_sem, idx_sem, *, tc):
    i = pl.program_id(0)
    slot = i % 2

    def start_gathers(idx_ref, s):
        cp = pltpu.make_async_copy(idx_ref, dest_smem.at[s], idx_sem)
        cp.start()
        cp.wait()

        def issue(t, carry):
            for k in range(TOP_K):
                d = dest_smem[s, 0, 0, TOP_K * t + k]
                pltpu.make_async_copy(y_hbm.at[pl.ds(d, 1)], ybuf.at[s, k, pl.ds(t, 1)],
                                      row_sem.at[s]).start()
            return carry

        lax.fori_loop(0, tc, issue, 0, unroll=8)

    @pl.when(i == 0)
    def _():
        start_gathers(dest_ref, 0)

    for s in range(2):
        @pl.when((i + 1 < pl.num_programs(0)) & (slot == 1 - s))
        def _():
            start_gathers(dest_next_ref, s)

    for k in range(TOP_K):
        pltpu.make_async_copy(y_hbm.at[pl.ds(0, tc)], ybuf.at[slot, k], row_sem.at[slot]).wait()

    meta = meta_ref[...]
    ffn = meta[:, 4:5] * ybuf[slot, 0] + meta[:, 5:6] * ybuf[slot, 1]
    o_ref[...] = _layer_norm(DEEPNORM_ALPHA * h_ref[...] + ffn, g2_ref[...], b2_ref[...])


def _combine(dest3, h, meta, g2, b2, y, *, tc):
    n_tok, d = h.shape
    n_steps = n_tok // tc
    return pl.pallas_call(
        functools.partial(_combine_kernel, tc=tc),
        out_shape=jax.ShapeDtypeStruct((n_tok, d), F32),
        grid=(n_steps,),
        in_specs=[pl.BlockSpec((1, 1, TOP_K * tc), lambda i: (i, 0, 0)),
                  pl.BlockSpec((1, 1, TOP_K * tc),
                               lambda i: (jnp.minimum(i + 1, n_steps - 1), 0, 0)),
                  pl.BlockSpec((tc, d), lambda i: (i, 0)),
                  pl.BlockSpec((tc, LANES), lambda i: (i, 0)),
                  pl.BlockSpec(g2.shape, lambda i: (0, 0)),
                  pl.BlockSpec(b2.shape, lambda i: (0, 0)),
                  pl.BlockSpec(memory_space=pl.ANY)],
        out_specs=pl.BlockSpec((tc, d), lambda i: (i, 0)),
        scratch_shapes=[pltpu.SMEM((2, 1, 1, TOP_K * tc), jnp.int32),
                        pltpu.VMEM((2, TOP_K, tc, d), F32),
                        pltpu.SemaphoreType.DMA((2,)),
                        pltpu.SemaphoreType.DMA],
        compiler_params=pltpu.CompilerParams(
            dimension_semantics=("arbitrary",), vmem_limit_bytes=V7X_VMEM_LIMIT),
        name="combine",
    )(dest3, dest3, h, meta, g2, b2, y)


def _w_in_column_order():
    hgw = HG_HEADS * HG_DK
    qkw = DF_HEADS * DF_DQK
    base = 4 * hgw
    cols = list(range(base))
    for first in (base, base + 2 * qkw):
        for h in range(DF_HEADS):
            cols += list(range(first + h * DF_DQK, first + (h + 1) * DF_DQK))
            cols += list(range(first + qkw + h * DF_DQK, first + qkw + (h + 1) * DF_DQK))
    total = base + 4 * qkw + DF_HEADS * DF_DV + MEM_HEADS * MEM_DH
    cols += list(range(base + 4 * qkw, total))
    return jnp.asarray(cols, jnp.int32)


def kernel(x, mem, positions, w_in, w_gates, hgrn_lower_bounds, hgrn_norm_gain, diff_lambda_q1, diff_lambda_k1, diff_lambda_q2, diff_lambda_k2, diff_subln_gain, w_mem_kv, w_branch_hgrn, w_branch_diff, w_branch_mem, w_out, ln1_gain, ln1_bias, w_group_router, w_expert_router, w_expert_gate, w_expert_up, w_expert_down, ln2_gain, ln2_bias):
    b, s, d = x.shape
    n_tok = b * s
    t = _tiles(s, n_tok)
    assert w_in.shape[0] == DEPTH
    layer = 0
    lam_init = 0.8 - 0.6 * math.exp(-0.3 * layer)
    hgw = HG_HEADS * HG_DK

    w_in_b = jnp.take(w_in[layer], _w_in_column_order(), axis=1).astype(BF16)
    inv_freq = ROPE_THETA ** (-jnp.arange(0, ROT_DIM, 2, dtype=F32) / ROT_DIM)
    invf8 = inv_freq.reshape(ROT_DIM // 2, 1)
    w_router = jnp.concatenate(
        [w_group_router[layer], w_expert_router[layer],
         jnp.zeros((d, LANES - N_GROUPS - N_EXPERTS), F32)], axis=1)
    wr_hi = w_router.astype(BF16)
    wr_lo = (w_router - wr_hi.astype(F32)).astype(BF16)
    wr_cat = jnp.concatenate([wr_hi, wr_lo], axis=1)

    x2 = x.reshape(n_tok, d)
    pos3 = positions.reshape(n_tok // t["proj_rows"], 1, t["proj_rows"])
    p, logf = _proj(x2, pos3, w_in_b, hgrn_lower_bounds, invf8, tm=t["proj_rows"], layer=layer)
    p3 = p.reshape(b, s, p.shape[1])

    y_hg = _hgrn(p3, logf.reshape(b, s, hgw), hgrn_norm_gain[layer].reshape(1, hgw),
                 rows=t["hgrn_rows"], par=t["hgrn_par"])
    y_df = _diffattn(p3, diff_lambda_q1[layer].reshape(1, -1), diff_lambda_k1[layer].reshape(1, -1),
                     diff_lambda_q2[layer].reshape(1, -1), diff_lambda_k2[layer].reshape(1, -1),
                     diff_subln_gain[layer].reshape(1, -1), tq=t["attn_q"], tk=t["attn_k"], rq=t["attn_rows"],
                     lam_init=lam_init, col0=4 * hgw // LANES)
    y_mem = _memattn(p3, mem, w_mem_kv[layer].astype(BF16), tq=t["mem_q"],
                     col0=7 * hgw // (MEM_HEADS * MEM_DH))

    h, meta, counts_f = _merge(
        x2, y_hg.reshape(n_tok, -1), y_df.reshape(n_tok, -1), y_mem.reshape(n_tok, -1),
        w_gates[layer].astype(BF16), w_branch_hgrn[layer].astype(BF16),
        w_branch_diff[layer].astype(BF16), w_branch_mem[layer].astype(BF16),
        w_out[layer].astype(BF16), ln1_gain[layer].reshape(1, d), ln1_bias[layer].reshape(1, d),
        wr_cat, tm=t["merge_rows"], parts=t["merge_parts"])

    rows = t["moe_rows"]
    counts = counts_f[0, N_GROUPS:N_GROUPS + N_EXPERTS].astype(jnp.int32)
    padded = (counts + rows - 1) // rows * rows
    padded_end = jnp.cumsum(padded)
    padded_start = padded_end - padded
    expert_id = meta[:, 0:TOP_K].astype(jnp.int32)
    rank = meta[:, TOP_K:2 * TOP_K].astype(jnp.int32)
    dest = padded_start[expert_id] + rank
    n_slots = n_tok * TOP_K + N_EXPERTS * rows
    n_blocks = n_slots // rows
    block_start = jnp.arange(n_blocks, dtype=jnp.int32) * rows
    block_expert = jnp.minimum(
        jnp.sum((padded_end[None, :] <= block_start[:, None]).astype(jnp.int32), axis=1),
        N_EXPERTS - 1)
    n_valid = (padded_end[-1:] // rows).astype(jnp.int32)

    ts = t["scatter_rows"]
    xs = _scatter(padded_end.astype(jnp.int32), padded, n_valid,
                  dest.reshape(n_tok // ts, 1, TOP_K * ts), h, ts=ts, rows=rows, n_slots=n_slots)
    y = _experts(block_expert, n_valid, xs, w_expert_gate[layer].astype(BF16),
                 w_expert_up[layer].astype(BF16), w_expert_down[layer].astype(BF16), rows=rows)
    tc = t["combine_rows"]
    out = _combine(dest.reshape(n_tok // tc, 1, TOP_K * tc), h, meta,
                   ln2_gain[layer].reshape(1, d), ln2_bias[layer].reshape(1, d), y, tc=tc)
    return out.reshape(b, s, d)
```

```python
import functools
import math

import jax
import jax.numpy as jnp
from jax import lax
from jax.experimental import pallas as pl
from jax.experimental.pallas import tpu as pltpu

F32 = jnp.float32
BF16 = jnp.bfloat16

HG_HEADS = 4
HG_DK = 128
HG_CHUNK = 64
DF_HEADS = 4
DF_DQK = 64
DF_DV = 128
ROPE_THETA = 500000.0
ROT_DIM = 16
MEM_HEADS = 4
MEM_DH = 128
N_BRANCH = 3
N_GROUPS = 4
EXPERTS_PER_GROUP = 8
N_EXPERTS = 32
TOP_K = 2
DEPTH = 1
DEEPNORM_ALPHA = (2.0 * DEPTH) ** 0.25
LN_EPS = 1e-5
RMS_EPS = 1e-6
LANES = 128
SUBLANES = 8
V7X_VMEM_LIMIT = 56 * 1024 * 1024


def _tiles(seq, n_tok):
    return dict(
        proj_rows=min(512, n_tok),
        hgrn_rows=min(512, seq),
        hgrn_par=4,
        attn_q=min(1024, seq),
        attn_k=min(512, seq),
        attn_rows=128,
        mem_q=min(512, seq),
        merge_rows=min(1024, n_tok),
        merge_parts=4,
        scatter_rows=min(1024, n_tok),
        moe_rows=512,
        combine_rows=min(256, n_tok),
    )


def _sigmoid(v):
    return 1.0 / (1.0 + jnp.exp(-v))


def _dot(a, b):
    return jnp.dot(a, b, preferred_element_type=F32)


def _dot_nt(a, b):
    return lax.dot_general(a, b, (((1,), (1,)), ((), ())), preferred_element_type=F32)


def _dot_tn(a, b):
    return lax.dot_general(a, b, (((0,), (0,)), ((), ())), preferred_element_type=F32)


def _proj_kernel(x_ref, pos_ref, w_ref, lbraw_ref, invf_ref, p_ref, logf_ref, *, layer):
    tm = x_ref.shape[0]
    wd = logf_ref.shape[1]
    x = x_ref[...].astype(BF16)

    def mm(j):
        return _dot(x, w_ref[:, j * wd:(j + 1) * wd])

    a = lbraw_ref[...]
    e = jnp.exp(a - jnp.max(a, axis=0, keepdims=True))
    sm = e / jnp.sum(e, axis=0, keepdims=True)
    lb = jnp.sum(sm[0:layer + 1, :], axis=0, keepdims=True)

    hq = mm(0)
    p_ref[:, 0:wd] = (hq * _sigmoid(hq)).astype(BF16)
    hf = mm(1)
    forget = lb + (1.0 - lb) * _sigmoid(hf)
    logf_ref[...] = jnp.log(forget)
    p_ref[:, wd:2 * wd] = (1.0 - forget).astype(BF16)
    p_ref[:, 2 * wd:3 * wd] = mm(2).astype(BF16)
    hg = mm(3)
    p_ref[:, 3 * wd:4 * wd] = (hg * _sigmoid(hg)).astype(BF16)

    ang = invf_ref[...] * pos_ref[...].astype(F32)
    c8 = jnp.cos(ang)
    s8 = jnp.sin(ang)
    one = jnp.ones((DF_DQK - ROT_DIM, tm), F32)
    zero = jnp.zeros((DF_DQK - ROT_DIM, tm), F32)
    z8 = jnp.zeros_like(s8)
    cos_t = jnp.concatenate([c8, c8, one, c8, c8, one], axis=0).T
    sin_lo = jnp.concatenate([-s8, z8, zero, -s8, z8, zero], axis=0).T
    sin_hi = jnp.concatenate([z8, s8, zero, z8, s8, zero], axis=0).T
    half = ROT_DIM // 2

    def rope(t):
        return (t * cos_t + pltpu.roll(t, LANES - half, 1) * sin_lo
                + pltpu.roll(t, half, 1) * sin_hi)

    q = mm(4)
    k = mm(5)
    for j in range(wd // LANES):
        sl = slice(j * LANES, (j + 1) * LANES)
        p_ref[:, 4 * wd + j * LANES:4 * wd + (j + 1) * LANES] = (
            rope(q[:, sl]) * (DF_DQK ** -0.5 * math.log2(math.e))).astype(BF16)
        p_ref[:, 5 * wd + j * LANES:5 * wd + (j + 1) * LANES] = rope(k[:, sl]).astype(BF16)
    p_ref[:, 6 * wd:7 * wd] = mm(6).astype(BF16)
    p_ref[:, 7 * wd:8 * wd] = mm(7).astype(BF16)


def _proj(x2, pos3, w_in_b, lbraw, invf8, *, tm, layer):
    n_tok, d = x2.shape
    width = w_in_b.shape[1]
    wd = lbraw.shape[1]
    return pl.pallas_call(
        functools.partial(_proj_kernel, layer=layer),
        out_shape=(jax.ShapeDtypeStruct((n_tok, width), BF16),
                   jax.ShapeDtypeStruct((n_tok, wd), F32)),
        grid=(n_tok // tm,),
        in_specs=[
            pl.BlockSpec((tm, d), lambda i: (i, 0)),
            pl.BlockSpec((None, 1, tm), lambda i: (i, 0, 0)),
            pl.BlockSpec((d, width), lambda i: (0, 0)),
            pl.BlockSpec(lbraw.shape, lambda i: (0, 0)),
            pl.BlockSpec(invf8.shape, lambda i: (0, 0)),
        ],
        out_specs=(pl.BlockSpec((tm, width), lambda i: (i, 0)),
                   pl.BlockSpec((tm, wd), lambda i: (i, 0))),
        compiler_params=pltpu.CompilerParams(
            dimension_semantics=("arbitrary",), vmem_limit_bytes=V7X_VMEM_LIMIT),
        name="proj",
    )(x2, pos3, w_in_b, lbraw, invf8)


def _hgrn_kernel(q_ref, k_ref, v_ref, g_ref, lf_ref, gain_ref, lvl_ref, tri_ref, o_ref, st_ref, *,
                 par):
    rows = q_ref.shape[0]
    c = HG_CHUNK
    dk = HG_DK

    @pl.when(pl.program_id(1) == 0)
    def _():
        st_ref[...] = jnp.zeros_like(st_ref)

    lvl = lvl_ref[...]
    tri = tri_ref[...]
    sub = lax.broadcasted_iota(jnp.int32, (c // 8, 8, dk), 1)
    row = lax.broadcasted_iota(jnp.int32, (c, dk), 0)

    def anchors(cum):
        out = []
        for m in (32, 16, 8):
            pieces = []
            for j in range(c // (2 * m)):
                a = j * 2 * m + m - 1
                pieces.append(jnp.broadcast_to(cum[a:a + 1, :], (2 * m, dk)))
            out.append(pieces[0] if len(pieces) == 1 else jnp.concatenate(pieces, axis=0))
        c8 = cum.reshape(c // 8, 8, dk)
        out.append(jnp.broadcast_to(c8[:, 3:4, :], c8.shape).reshape(c, dk))
        a2 = jnp.where(sub < 4, jnp.broadcast_to(c8[:, 1:2, :], c8.shape),
                       jnp.broadcast_to(c8[:, 5:6, :], c8.shape))
        out.append(a2.reshape(c, dk))
        return out

    def chunk_body(ci, carry):
        units = [(cc, h) for cc in range(par) for h in range(HG_HEADS)]
        r0 = [pl.multiple_of((ci * par + cc) * c, c) for cc in range(par)]
        hsl = [slice(h * dk, (h + 1) * dk) for h in range(HG_HEADS)]
        q = {u: q_ref[pl.ds(r0[u[0]], c), hsl[u[1]]].astype(F32) for u in units}
        k = {u: k_ref[pl.ds(r0[u[0]], c), hsl[u[1]]].astype(F32) for u in units}
        v = {u: v_ref[pl.ds(r0[u[0]], c), hsl[u[1]]] for u in units}
        lf = {u: lf_ref[pl.ds(r0[u[0]], c), hsl[u[1]]] for u in units}
        cum = {}
        for u in units:
            lf1 = lf[u].astype(BF16)
            r1 = lf[u] - lf1.astype(F32)
            lf2 = r1.astype(BF16)
            lf3 = (r1 - lf2.astype(F32)).astype(BF16)
            cum[u] = _dot(tri, lf1) + _dot(tri, lf2) + _dot(tri, lf3)
        operands = {}
        for u in units:
            e_lvls = [jnp.exp(-jnp.abs(cum[u] - a)) for a in anchors(cum[u])]
            e_lvls.append(jnp.where((row & 1) == 1, jnp.exp(lf[u]), 1.0))
            ops = [((q[u] * e).astype(BF16), (k[u] * e).astype(BF16)) for e in e_lvls]
            ops.append((q[u].astype(BF16), k[u].astype(BF16)))
            operands[u] = ops
        ecum = {u: jnp.exp(cum[u]) for u in units}
        k_dec = {u: (k[u] * jnp.exp(cum[u][c - 1:c, :] - cum[u])).astype(BF16) for u in units}
        q_dec = {u: (q[u] * ecum[u]).astype(BF16) for u in units}
        level_dots = {u: [_dot_nt(qe, ke) for qe, ke in operands[u]] for u in units}
        st = [st_ref[h] for h in range(HG_HEADS)]
        o_inter = {}
        for u in units:
            h = u[1]
            o_inter[u] = _dot_nt(q_dec[u], st[h].astype(BF16))
            st[h] = st[h] * ecum[u][c - 1:c, :] + _dot_tn(v[u], k_dec[u])
        for h in range(HG_HEADS):
            st_ref[h] = st[h]
        scores = {}
        for u in units:
            sc = jnp.zeros((c, c), F32)
            for i, d in enumerate(level_dots[u]):
                sc = jnp.where(lvl == i, d, sc)
            scores[u] = sc.astype(BF16)
        for u in units:
            o = o_inter[u] + _dot(scores[u], v[u])
            y = o * lax.rsqrt(jnp.mean(o * o, axis=-1, keepdims=True) + RMS_EPS)
            y = y * gain_ref[:, hsl[u[1]]] * g_ref[pl.ds(r0[u[0]], c), hsl[u[1]]].astype(F32)
            o_ref[pl.ds(r0[u[0]], c), hsl[u[1]]] = y.astype(o_ref.dtype)
        return carry

    lax.fori_loop(0, rows // (c * par), chunk_body, 0)


def _hgrn_consts():
    c = HG_CHUNK
    t = jnp.arange(c)[:, None]
    s = jnp.arange(c)[None, :]
    x = t ^ s
    lvl = jnp.full((c, c), -1, jnp.int32)
    for i, m in enumerate((32, 16, 8, 4, 2, 1)):
        lvl = jnp.where((t > s) & (x >= m) & (x < 2 * m), i, lvl)
    lvl = jnp.where(t == s, 6, lvl)
    tri = (t >= s).astype(BF16)
    return lvl, tri


def _hgrn(p3, logf3, gain, *, rows, par):
    b, s, _ = p3.shape
    wd = logf3.shape[2]
    lvl, tri = _hgrn_consts()

    def col(j):
        return pl.BlockSpec((None, rows, wd), lambda bi, si: (bi, si, j))

    return pl.pallas_call(
        functools.partial(_hgrn_kernel, par=par),
        out_shape=jax.ShapeDtypeStruct((b, s, wd), BF16),
        grid=(b, s // rows),
        in_specs=[col(0), col(1), col(2), col(3), col(0),
                  pl.BlockSpec(gain.shape, lambda bi, si: (0, 0)),
                  pl.BlockSpec(lvl.shape, lambda bi, si: (0, 0)),
                  pl.BlockSpec(tri.shape, lambda bi, si: (0, 0))],
        out_specs=col(0),
        scratch_shapes=[pltpu.VMEM((HG_HEADS, HG_DK, HG_DK), F32)],
        compiler_params=pltpu.CompilerParams(
            dimension_semantics=("arbitrary", "arbitrary"), vmem_limit_bytes=V7X_VMEM_LIMIT),
        name="hgrn",
    )(p3, p3, p3, p3, logf3, gain, lvl, tri)


def _diffattn_kernel(q_ref, k_ref, v_ref, lq1_ref, lk1_ref, lq2_ref, lk2_ref, gain_ref, o_ref,
                     m_ref, l_ref, acc_ref, *, tk, rq, lam_init):
    tq = q_ref.shape[0]
    qi = pl.program_id(2)
    lane = lax.broadcasted_iota(jnp.int32, (tq, LANES), 1)
    q = q_ref[...]
    zero = jnp.zeros_like(q)
    qm = (jnp.where(lane < DF_DQK, q, zero), jnp.where(lane >= DF_DQK, q, zero))

    m_ref[...] = jnp.full(m_ref.shape, -jnp.inf, F32)
    l_ref[...] = jnp.zeros(l_ref.shape, F32)
    acc_ref[...] = jnp.zeros(acc_ref.shape, F32)

    def tile(k0, diag):
        blocks = [(r, j) for r in range(tq // rq) for j in range(2)]
        n_keys = {r: ((r + 1) * rq if diag else tk) for r in range(tq // rq)}
        scores = {}
        for r, j in blocks:
            rows = slice(r * rq, (r + 1) * rq)
            scores[r, j] = _dot_nt(qm[j][rows], k_ref[pl.ds(k0, n_keys[r]), :])
        probs = {}
        for r, j in blocks:
            rows = slice(r * rq, (r + 1) * rq)
            s = scores[r, j]
            chunks = [s[:, c * LANES:(c + 1) * LANES] for c in range(n_keys[r] // LANES)]
            if diag:
                n_d = rq // LANES
                rr = lax.broadcasted_iota(jnp.int32, (rq, LANES), 0)
                cc = lax.broadcasted_iota(jnp.int32, (rq, LANES), 1)
                for c in range(n_d):
                    idx = len(chunks) - n_d + c
                    chunks[idx] = jnp.where(rr >= cc + c * LANES, chunks[idx], -jnp.inf)
            cmax = functools.reduce(jnp.maximum, chunks)
            m_old = m_ref[j, rows]
            m_new = jnp.maximum(m_old, jnp.max(cmax, axis=-1, keepdims=True))
            alpha = jnp.exp2(m_old - m_new)
            ps = [jnp.exp2(ch - m_new) for ch in chunks]
            l_ref[j, rows] = alpha * l_ref[j, rows] + functools.reduce(lambda a, b: a + b, ps)
            m_ref[j, rows] = m_new
            acc_ref[j, rows] = alpha * acc_ref[j, rows]
            probs[r, j] = jnp.concatenate([x.astype(BF16) for x in ps], axis=1)
        for r, j in blocks:
            rows = slice(r * rq, (r + 1) * rq)
            acc_ref[j, rows] += _dot(probs[r, j], v_ref[pl.ds(k0, n_keys[r]), :])

    def body(ki, carry):
        tile(pl.multiple_of(ki * tk, tk), False)
        return carry

    lax.fori_loop(0, (qi * tq) // tk, body, 0)
    tile(pl.multiple_of(qi * tq, tq), True)

    lam = (jnp.exp(jnp.sum(lq1_ref[...] * lk1_ref[...], keepdims=True))
           - jnp.exp(jnp.sum(lq2_ref[...] * lk2_ref[...], keepdims=True)) + lam_init)
    l0 = jnp.sum(l_ref[0], axis=-1, keepdims=True)
    l1 = jnp.sum(l_ref[1], axis=-1, keepdims=True)
    o = acc_ref[0] / l0 - lam * (acc_ref[1] / l1)
    y = o * lax.rsqrt(jnp.mean(o * o, axis=-1, keepdims=True) + RMS_EPS)
    o_ref[...] = (y * gain_ref[...] * (1.0 - lam_init)).astype(o_ref.dtype)


def _diffattn(p3, lq1, lk1, lq2, lk2, gain, *, tq, tk, rq, lam_init, col0):
    b, s, _ = p3.shape
    assert tq % tk == 0 and tq % rq == 0 and rq % LANES == 0
    small = pl.BlockSpec((1, DF_DQK), lambda bi, hi, qi: (0, 0))
    return pl.pallas_call(
        functools.partial(_diffattn_kernel, tk=tk, rq=rq, lam_init=lam_init),
        out_shape=jax.ShapeDtypeStruct((b, s, DF_HEADS * DF_DV), BF16),
        grid=(b, DF_HEADS, s // tq),
        in_specs=[
            pl.BlockSpec((None, tq, LANES), lambda bi, hi, qi: (bi, qi, col0 + hi)),
            pl.BlockSpec((None, s, LANES), lambda bi, hi, qi: (bi, 0, col0 + DF_HEADS + hi)),
            pl.BlockSpec((None, s, LANES), lambda bi, hi, qi: (bi, 0, col0 + 2 * DF_HEADS + hi)),
            small, small, small, small,
            pl.BlockSpec((1, DF_DV), lambda bi, hi, qi: (0, 0)),
        ],
        out_specs=pl.BlockSpec((None, tq, DF_DV), lambda bi, hi, qi: (bi, qi, hi)),
        scratch_shapes=[pltpu.VMEM((2, tq, LANES), F32), pltpu.VMEM((2, tq, LANES), F32),
                        pltpu.VMEM((2, tq, DF_DV), F32)],
        compiler_params=pltpu.CompilerParams(
            dimension_semantics=("arbitrary", "arbitrary", "arbitrary"),
            vmem_limit_bytes=V7X_VMEM_LIMIT),
        name="diffattn",
    )(p3, p3, p3, lq1, lk1, lq2, lk2, gain)


def _memattn_kernel(q_ref, mem_ref, w_ref, o_ref, kv_ref):
    width = MEM_HEADS * MEM_DH

    @pl.when(pl.program_id(1) == 0)
    def _():
        kv_ref[...] = _dot(mem_ref[...].astype(BF16), w_ref[...]).astype(BF16)

    for h in range(MEM_HEADS):
        hs = slice(h * MEM_DH, (h + 1) * MEM_DH)
        s = _dot_nt(q_ref[:, hs], kv_ref[:, hs]) * (MEM_DH ** -0.5)
        e = jnp.exp(s - jnp.max(s, axis=-1, keepdims=True))
        p = e / jnp.sum(e, axis=-1, keepdims=True)
        o_ref[:, hs] = _dot(p.astype(BF16), kv_ref[:, width + h * MEM_DH:width + (h + 1) * MEM_DH]
                            ).astype(o_ref.dtype)


def _memattn(p3, mem, w_kv_b, *, tq, col0):
    b, s, _ = p3.shape
    n_mem, d = mem.shape[1], mem.shape[2]
    width = MEM_HEADS * MEM_DH
    return pl.pallas_call(
        _memattn_kernel,
        out_shape=jax.ShapeDtypeStruct((b, s, width), BF16),
        grid=(b, s // tq),
        in_specs=[
            pl.BlockSpec((None, tq, width), lambda bi, qi: (bi, qi, col0)),
            pl.BlockSpec((None, n_mem, d), lambda bi, qi: (bi, 0, 0)),
            pl.BlockSpec(w_kv_b.shape, lambda bi, qi: (0, 0)),
        ],
        out_specs=pl.BlockSpec((None, tq, width), lambda bi, qi: (bi, qi, 0)),
        scratch_shapes=[pltpu.VMEM((n_mem, 2 * width), BF16)],
        compiler_params=pltpu.CompilerParams(
            dimension_semantics=("arbitrary", "arbitrary"), vmem_limit_bytes=V7X_VMEM_LIMIT),
        name="memattn",
    )(p3, mem, w_kv_b)


def _layer_norm(z, gain, bias):
    mu = jnp.mean(z, axis=-1, keepdims=True)
    zc = z - mu
    var = jnp.mean(zc * zc, axis=-1, keepdims=True)
    return zc * lax.rsqrt(var + LN_EPS) * gain + bias


def _merge_kernel(x_ref, yh_ref, yd_ref, ym_ref, wg_ref, wbh_ref, wbd_ref, wbm_ref, wo_ref,
                  g1_ref, b1_ref, wr_cat_ref, tri_ref, h_ref, meta_ref, meta_t_ref, cnt_ref, run_ref,
                  *, parts):
    tm, d = x_ref.shape
    rp = tm // parts

    @pl.when(pl.program_id(0) == 0)
    def _():
        run_ref[...] = jnp.zeros_like(run_ref)

    lane = lax.broadcasted_iota(jnp.int32, (rp, LANES), 1)
    neg = jnp.full((rp, LANES), -jnp.inf, F32)
    big = jnp.full((rp, LANES), 4 * LANES, jnp.int32)
    is_g = lane < N_GROUPS
    branches = ((yh_ref, wbh_ref), (yd_ref, wbd_ref), (ym_ref, wbm_ref))

    def project(p):
        rows = slice(p * rp, (p + 1) * rp)
        x = x_ref[rows, :]
        xb = x.astype(BF16)
        merged = None
        for j, (y_ref, wb_ref) in enumerate(branches):
            gate = _sigmoid(_dot(xb, wg_ref[:, j * d:(j + 1) * d]))
            term = gate * _dot(y_ref[rows, :], wb_ref[...])
            merged = term if merged is None else merged + term
        return x, merged.astype(BF16)

    def norm(p, x, mix):
        h = _layer_norm(DEEPNORM_ALPHA * x + mix, g1_ref[...], b1_ref[...])
        h_ref[p * rp:(p + 1) * rp, :] = h
        h_hi = h.astype(BF16)
        h_lo = (h - h_hi.astype(F32)).astype(BF16)
        return h_hi, h_lo

    def logits(h_hi, h_lo):
        t = _dot(h_hi, wr_cat_ref[...])
        return t[:, :LANES] + t[:, LANES:] + _dot(h_lo, wr_cat_ref[:, :LANES])

    def route(lg):
        g_max = jnp.max(jnp.where(is_g, lg, neg), axis=-1, keepdims=True)
        g_sum = jnp.sum(jnp.where(is_g, jnp.exp(lg - g_max), 0.0), axis=-1, keepdims=True)
        group_w = 1.0 / g_sum
        g_idx = jnp.min(jnp.where(is_g & (lg == g_max), lane, big), axis=-1, keepdims=True)
        in_grp = ((lane >= N_GROUPS) & (lane < N_GROUPS + N_EXPERTS)
                  & (jnp.right_shift(lane - N_GROUPS, 3) == g_idx))
        v1 = jnp.max(jnp.where(in_grp, lg, neg), axis=-1, keepdims=True)
        i1 = jnp.min(jnp.where(in_grp & (lg == v1), lane, big), axis=-1, keepdims=True)
        rest = in_grp & (lane != i1)
        v2 = jnp.max(jnp.where(rest, lg, neg), axis=-1, keepdims=True)
        i2 = jnp.min(jnp.where(rest & (lg == v2), lane, big), axis=-1, keepdims=True)
        e21 = jnp.exp(v2 - v1)
        w1 = group_w / (1.0 + e21)
        w2 = group_w * e21 / (1.0 + e21)
        return i1, i2, w1, w2

    def finish(p, routed, run):
        i1, i2, w1, w2 = routed
        hot1 = lane == i1
        hot2 = lane == i2
        hot = jnp.where(hot1 | hot2, 1.0, 0.0)
        before = _dot(tri_ref[...], hot.astype(BF16)) + run
        r1 = jnp.sum(jnp.where(hot1, before, 0.0), axis=-1, keepdims=True)
        r2 = jnp.sum(jnp.where(hot2, before, 0.0), axis=-1, keepdims=True)
        meta = jnp.zeros((rp, LANES), F32)
        for col, val in enumerate(((i1 - N_GROUPS).astype(F32), (i2 - N_GROUPS).astype(F32),
                                   r1, r2, w1, w2)):
            meta = jnp.where(lane == col, val, meta)
        meta_ref[p * rp:(p + 1) * rp, :] = meta
        meta_t_ref[:, p * rp:(p + 1) * rp] = meta.T[:SUBLANES, :]
        return before[rp - 1:rp, :] + hot[rp - 1:rp, :]

    run = run_ref[...]
    state = {}
    for step in range(parts + 4):
        p = step - 4
        if 0 <= p < parts:
            run = finish(p, state.pop(("routed", p)), run)
        p = step - 3
        if 0 <= p < parts:
            state["routed", p] = route(state.pop(("lg", p)))
        p = step - 2
        if 0 <= p < parts:
            state["lg", p] = logits(*state.pop(("h", p)))
        p = step - 1
        if 0 <= p < parts:
            x, merged = state.pop(("proj", p))
            state["h", p] = norm(p, x, _dot(merged, wo_ref[...]))
        p = step
        if 0 <= p < parts:
            state["proj", p] = project(p)
    run_ref[...] = run
    cnt_ref[...] = run


def _merge(x2, yh, yd, ym, wg_b, wbh_b, wbd_b, wbm_b, wo_b, g1, b1, wr_cat, *, tm, parts):
    n_tok, d = x2.shape
    rp = tm // parts
    tri = (jnp.arange(rp)[:, None] > jnp.arange(rp)[None, :]).astype(BF16)
    row = lambda w: pl.BlockSpec((tm, w), lambda i: (i, 0))
    full = lambda a: pl.BlockSpec(a.shape, lambda i: (0, 0))
    return pl.pallas_call(
        functools.partial(_merge_kernel, parts=parts),
        out_shape=(jax.ShapeDtypeStruct((n_tok, d), F32),
                   jax.ShapeDtypeStruct((n_tok, LANES), F32),
                   jax.ShapeDtypeStruct((SUBLANES, n_tok), F32),
                   jax.ShapeDtypeStruct((1, LANES), F32)),
        grid=(n_tok // tm,),
        in_specs=[row(d), row(yh.shape[1]), row(yd.shape[1]), row(ym.shape[1]),
                  full(wg_b), full(wbh_b), full(wbd_b), full(wbm_b), full(wo_b),
                  full(g1), full(b1), full(wr_cat), full(tri)],
        out_specs=(row(d), row(LANES), pl.BlockSpec((SUBLANES, tm), lambda i: (0, i)),
                   pl.BlockSpec((1, LANES), lambda i: (0, 0))),
        scratch_shapes=[pltpu.VMEM((1, LANES), F32)],
        compiler_params=pltpu.CompilerParams(
            dimension_semantics=("arbitrary",), vmem_limit_bytes=V7X_VMEM_LIMIT),
        name="merge",
    )(x2, yh, yd, ym, wg_b, wbh_b, wbd_b, wbm_b, wo_b, g1, b1, wr_cat, tri)


def _scatter_kernel(pend_ref, psize_ref, nv_ref, dest_ref, h_ref, xs_hbm, dest_smem, zero_ref,
                    row_sem, aux_sem, *, ts, rows):
    cp = pltpu.make_async_copy(dest_ref, dest_smem, aux_sem)
    cp.start()
    cp.wait()
    n_blocks = xs_hbm.shape[0] * SUBLANES // rows

    @pl.when(pl.program_id(0) == 0)
    def _():
        zero_ref[...] = jnp.zeros_like(zero_ref)

        def fill(start):
            return pltpu.make_async_copy(
                zero_ref, xs_hbm.at[pl.ds(start // SUBLANES, rows // SUBLANES)], aux_sem)

        def fills(act):
            for e in range(N_EXPERTS):
                @pl.when(psize_ref[e] > 0)
                def _():
                    act(fill(pl.multiple_of(pend_ref[e] - rows, rows)))
            for blk in range(n_blocks - N_EXPERTS, n_blocks):
                @pl.when(blk >= nv_ref[0])
                def _():
                    act(fill(blk * rows))

        fills(lambda c: c.start())
        fills(lambda c: c.wait())

    def issue(g, carry):
        for u in range(SUBLANES):
            for k in range(TOP_K):
                d = dest_smem[0, 0, k * ts + g * SUBLANES + u]
                pltpu.make_async_copy(
                    h_ref.at[g, pl.ds(u, 1)],
                    xs_hbm.at[jnp.right_shift(d, 3), pl.ds(jnp.bitwise_and(d, SUBLANES - 1), 1)],
                    row_sem).start(priority=k)
        return carry

    lax.fori_loop(0, ts // SUBLANES, issue, 0)
    n_grp = TOP_K * ts // SUBLANES
    pltpu.make_async_copy(xs_hbm.at[pl.ds(0, n_grp)], xs_hbm.at[pl.ds(0, n_grp)], row_sem).wait()


def _scatter(padded_end, padded, n_valid, dest3, h, *, ts, rows, n_slots):
    n_tok, d = h.shape
    h = h.reshape(n_tok // SUBLANES, SUBLANES, d)
    return pl.pallas_call(
        functools.partial(_scatter_kernel, ts=ts, rows=rows),
        out_shape=jax.ShapeDtypeStruct((n_slots // SUBLANES, SUBLANES, d), h.dtype),
        grid_spec=pltpu.PrefetchScalarGridSpec(
            num_scalar_prefetch=3,
            grid=(n_tok // ts,),
            in_specs=[pl.BlockSpec((1, 1, TOP_K * ts), lambda i, pe, ps, nv: (i, 0, 0)),
                      pl.BlockSpec((ts // SUBLANES, SUBLANES, d), lambda i, pe, ps, nv: (i, 0, 0))],
            out_specs=pl.BlockSpec(memory_space=pl.ANY),
            scratch_shapes=[pltpu.SMEM((1, 1, TOP_K * ts), jnp.int32),
                            pltpu.VMEM((rows // SUBLANES, SUBLANES, d), h.dtype),
                            pltpu.SemaphoreType.DMA, pltpu.SemaphoreType.DMA],
        ),
        compiler_params=pltpu.CompilerParams(
            dimension_semantics=("arbitrary",), vmem_limit_bytes=V7X_VMEM_LIMIT),
        name="scatter",
    )(padded_end, padded, n_valid, dest3, h)


def _experts_kernel(be_ref, nv_ref, x_ref, wg_ref, wu_ref, wd_ref, y_ref, wg_b, wu_b, wd_b):
    j = pl.program_id(0)
    new_expert = (j == 0) | (be_ref[j] != be_ref[jnp.maximum(j - 1, 0)])

    @pl.when(new_expert & (j < nv_ref[0]))
    def _():
        wg_b[...] = wg_ref[...].astype(BF16)
        wu_b[...] = wu_ref[...].astype(BF16)
        wd_b[...] = wd_ref[...].astype(BF16)

    @pl.when(j < nv_ref[0])
    def _():
        half = x_ref.shape[0] // 2
        acts = []
        for p in range(2):
            x = x_ref[p * half:(p + 1) * half, :].astype(BF16)
            g = _dot(x, wg_b[...])
            u = _dot(x, wu_b[...])
            acts.append((g * _sigmoid(g) * u).astype(BF16))
        for p in range(2):
            y_ref[p * half:(p + 1) * half, :] = _dot(acts[p], wd_b[...])

    @pl.when(j >= nv_ref[0])
    def _():
        y_ref[...] = jnp.zeros_like(y_ref)


def _experts(block_expert, n_valid, xs, w_gate, w_up, w_down, *, rows):
    n_slots, d = xs.shape
    de = w_gate.shape[2]
    return pl.pallas_call(
        _experts_kernel,
        out_shape=jax.ShapeDtypeStruct((n_slots, d), F32),
        grid_spec=pltpu.PrefetchScalarGridSpec(
            num_scalar_prefetch=2,
            grid=(n_slots // rows,),
            in_specs=[pl.BlockSpec((rows, d), lambda j, be, nv: (jnp.minimum(j, nv[0] - 1), 0)),
                      pl.BlockSpec((None, d, de), lambda j, be, nv: (be[j], 0, 0)),
                      pl.BlockSpec((None, d, de), lambda j, be, nv: (be[j], 0, 0)),
                      pl.BlockSpec((None, de, d), lambda j, be, nv: (be[j], 0, 0))],
            out_specs=pl.BlockSpec((rows, d), lambda j, be, nv: (j, 0)),
            scratch_shapes=[pltpu.VMEM((d, de), BF16), pltpu.VMEM((d, de), BF16),
                            pltpu.VMEM((de, d), BF16)],
        ),
        compiler_params=pltpu.CompilerParams(
            dimension_semantics=("arbitrary",), vmem_limit_bytes=V7X_VMEM_LIMIT),
        name="experts",
    )(block_expert, n_valid, xs, w_gate, w_up, w_down)


def _combine_kernel(dest_ref, dest_next_ref, h_ref, meta_ref, g2_ref, b2_ref, y_hbm, o_ref,
                    dest_smem, ybuf, row_sem, idx_sem, *, tc):
    i = pl.program_id(0)
    slot = i % 2

    def start_gathers(idx_ref, s):
        cp = pltpu.make_async_copy(idx_ref, dest_smem.at[s], idx_sem)
        cp.start()
        cp.wait()

        def issue(g, carry):
            for u in range(SUBLANES):
                for k in range(TOP_K):
                    d = dest_smem[s, 0, 0, k * tc + g * SUBLANES + u]
                    pltpu.make_async_copy(
                        y_hbm.at[jnp.right_shift(d, 3), pl.ds(jnp.bitwise_and(d, SUBLANES - 1), 1)],
                        ybuf.at[s, k, g, pl.ds(u, 1)], row_sem.at[s]).start(priority=k)
            return carry

        lax.fori_loop(0, tc // SUBLANES, issue, 0)

    @pl.when(i == 0)
    def _():
        start_gathers(dest_ref, 0)

    for s in range(2):
        @pl.when((i + 1 < pl.num_programs(0)) & (slot == 1 - s))
        def _():
            start_gathers(dest_next_ref, s)

    for k in range(TOP_K):
        pltpu.make_async_copy(y_hbm.at[pl.ds(0, tc // SUBLANES)], ybuf.at[slot, k],
                              row_sem.at[slot]).wait()

    meta = meta_ref[...]
    d = h_ref.shape[1]
    ffn = (meta[:, 4:5] * ybuf[slot, 0].reshape(tc, d) + meta[:, 5:6] * ybuf[slot, 1].reshape(tc, d))
    o_ref[...] = _layer_norm(DEEPNORM_ALPHA * h_ref[...] + ffn, g2_ref[...], b2_ref[...])


def _combine(dest3, h, meta, g2, b2, y, *, tc):
    n_tok, d = h.shape
    n_steps = n_tok // tc
    y = y.reshape(y.shape[0] // SUBLANES, SUBLANES, d)
    return pl.pallas_call(
        functools.partial(_combine_kernel, tc=tc),
        out_shape=jax.ShapeDtypeStruct((n_tok, d), F32),
        grid=(n_steps,),
        in_specs=[pl.BlockSpec((1, 1, TOP_K * tc), lambda i: (i, 0, 0)),
                  pl.BlockSpec((1, 1, TOP_K * tc),
                               lambda i: (jnp.minimum(i + 1, n_steps - 1), 0, 0)),
                  pl.BlockSpec((tc, d), lambda i: (i, 0)),
                  pl.BlockSpec((tc, LANES), lambda i: (i, 0)),
                  pl.BlockSpec(g2.shape, lambda i: (0, 0)),
                  pl.BlockSpec(b2.shape, lambda i: (0, 0)),
                  pl.BlockSpec(memory_space=pl.ANY)],
        out_specs=pl.BlockSpec((tc, d), lambda i: (i, 0)),
        scratch_shapes=[pltpu.SMEM((2, 1, 1, TOP_K * tc), jnp.int32),
                        pltpu.VMEM((2, TOP_K, tc // SUBLANES, SUBLANES, d), F32),
                        pltpu.SemaphoreType.DMA((2,)),
                        pltpu.SemaphoreType.DMA],
        compiler_params=pltpu.CompilerParams(
            dimension_semantics=("arbitrary",), vmem_limit_bytes=V7X_VMEM_LIMIT),
        name="combine",
    )(dest3, dest3, h, meta, g2, b2, y)


def _w_in_column_order():
    hgw = HG_HEADS * HG_DK
    qkw = DF_HEADS * DF_DQK
    base = 4 * hgw
    cols = list(range(base))
    for first in (base, base + 2 * qkw):
        for h in range(DF_HEADS):
            cols += list(range(first + h * DF_DQK, first + (h + 1) * DF_DQK))
            cols += list(range(first + qkw + h * DF_DQK, first + qkw + (h + 1) * DF_DQK))
    total = base + 4 * qkw + DF_HEADS * DF_DV + MEM_HEADS * MEM_DH
    cols += list(range(base + 4 * qkw, total))
    return jnp.asarray(cols, jnp.int32)


def kernel(x, mem, positions, w_in, w_gates, hgrn_lower_bounds, hgrn_norm_gain, diff_lambda_q1, diff_lambda_k1, diff_lambda_q2, diff_lambda_k2, diff_subln_gain, w_mem_kv, w_branch_hgrn, w_branch_diff, w_branch_mem, w_out, ln1_gain, ln1_bias, w_group_router, w_expert_router, w_expert_gate, w_expert_up, w_expert_down, ln2_gain, ln2_bias):
    b, s, d = x.shape
    n_tok = b * s
    t = _tiles(s, n_tok)
    assert w_in.shape[0] == DEPTH
    layer = 0
    lam_init = 0.8 - 0.6 * math.exp(-0.3 * layer)
    hgw = HG_HEADS * HG_DK

    w_in_b = jnp.take(w_in[layer], _w_in_column_order(), axis=1).astype(BF16)
    inv_freq = ROPE_THETA ** (-jnp.arange(0, ROT_DIM, 2, dtype=F32) / ROT_DIM)
    invf8 = inv_freq.reshape(ROT_DIM // 2, 1)
    w_router = jnp.concatenate(
        [w_group_router[layer], w_expert_router[layer],
         jnp.zeros((d, LANES - N_GROUPS - N_EXPERTS), F32)], axis=1)
    wr_hi = w_router.astype(BF16)
    wr_lo = (w_router - wr_hi.astype(F32)).astype(BF16)
    wr_cat = jnp.concatenate([wr_hi, wr_lo], axis=1)

    x2 = x.reshape(n_tok, d)
    pos3 = positions.reshape(n_tok // t["proj_rows"], 1, t["proj_rows"])
    p, logf = _proj(x2, pos3, w_in_b, hgrn_lower_bounds, invf8, tm=t["proj_rows"], layer=layer)
    p3 = p.reshape(b, s, p.shape[1])

    y_hg = _hgrn(p3, logf.reshape(b, s, hgw), hgrn_norm_gain[layer].reshape(1, hgw),
                 rows=t["hgrn_rows"], par=t["hgrn_par"])
    y_df = _diffattn(p3, diff_lambda_q1[layer].reshape(1, -1), diff_lambda_k1[layer].reshape(1, -1),
                     diff_lambda_q2[layer].reshape(1, -1), diff_lambda_k2[layer].reshape(1, -1),
                     diff_subln_gain[layer].reshape(1, -1), tq=t["attn_q"], tk=t["attn_k"], rq=t["attn_rows"],
                     lam_init=lam_init, col0=4 * hgw // LANES)
    y_mem = _memattn(p3, mem, w_mem_kv[layer].astype(BF16), tq=t["mem_q"],
                     col0=7 * hgw // (MEM_HEADS * MEM_DH))

    h, meta, meta_t, counts_f = _merge(
        x2, y_hg.reshape(n_tok, -1), y_df.reshape(n_tok, -1), y_mem.reshape(n_tok, -1),
        w_gates[layer].astype(BF16), w_branch_hgrn[layer].astype(BF16),
        w_branch_diff[layer].astype(BF16), w_branch_mem[layer].astype(BF16),
        w_out[layer].astype(BF16), ln1_gain[layer].reshape(1, d), ln1_bias[layer].reshape(1, d),
        wr_cat, tm=t["merge_rows"], parts=t["merge_parts"])

    rows = t["moe_rows"]
    counts = counts_f[0, N_GROUPS:N_GROUPS + N_EXPERTS].astype(jnp.int32)
    padded = (counts + rows - 1) // rows * rows
    padded_end = jnp.cumsum(padded)
    padded_start = padded_end - padded
    expert_id = meta_t[0:TOP_K].astype(jnp.int32)
    rank = meta_t[TOP_K:2 * TOP_K].astype(jnp.int32)
    dest = padded_start[expert_id] + rank

    def dest_tiles(tile):
        return dest.reshape(TOP_K, n_tok // tile, tile).transpose(1, 0, 2).reshape(
            n_tok // tile, 1, TOP_K * tile)
    n_slots = n_tok * TOP_K + N_EXPERTS * rows
    n_blocks = n_slots // rows
    block_start = jnp.arange(n_blocks, dtype=jnp.int32) * rows
    block_expert = jnp.minimum(
        jnp.sum((padded_end[None, :] <= block_start[:, None]).astype(jnp.int32), axis=1),
        N_EXPERTS - 1)
    n_valid = (padded_end[-1:] // rows).astype(jnp.int32)

    ts = t["scatter_rows"]
    xs = _scatter(padded_end.astype(jnp.int32), padded, n_valid,
                  dest_tiles(ts), h, ts=ts, rows=rows, n_slots=n_slots)
    y = _experts(block_expert, n_valid, xs.reshape(n_slots, d), w_expert_gate[layer],
                 w_expert_up[layer], w_expert_down[layer], rows=rows)
    tc = t["combine_rows"]
    out = _combine(dest_tiles(tc), h, meta,
                   ln2_gain[layer].reshape(1, d), ln2_bias[layer].reshape(1, d), y, tc=tc)
    return out.reshape(b, s, d)
```

```python
import functools
import math

import jax
import jax.numpy as jnp
from jax import lax
from jax.experimental import pallas as pl
from jax.experimental.pallas import tpu as pltpu

F32 = jnp.float32
BF16 = jnp.bfloat16

HG_HEADS = 4
HG_DK = 128
HG_CHUNK = 64
DF_HEADS = 4
DF_DQK = 64
DF_DV = 128
ROPE_THETA = 500000.0
ROT_DIM = 16
MEM_HEADS = 4
MEM_DH = 128
N_BRANCH = 3
N_GROUPS = 4
EXPERTS_PER_GROUP = 8
N_EXPERTS = 32
TOP_K = 2
DEPTH = 1
DEEPNORM_ALPHA = (2.0 * DEPTH) ** 0.25
LN_EPS = 1e-5
RMS_EPS = 1e-6
LANES = 128
SUBLANES = 8
V7X_VMEM_LIMIT = 56 * 1024 * 1024


def _tiles(seq, n_tok):
    return dict(
        proj_rows=min(1024, n_tok),
        hgrn_rows=min(512, seq),
        hgrn_par=4,
        attn_q=min(2048, seq),
        attn_k=min(512, seq),
        attn_rows=128,
        mem_q=min(512, seq),
        merge_rows=min(1024, n_tok),
        merge_parts=4,
        scatter_rows=min(1024, n_tok),
        moe_rows=512,
        combine_rows=min(256, n_tok),
    )


def _sigmoid(v):
    return 1.0 / (1.0 + jnp.exp(-v))


def _dot(a, b):
    return jnp.dot(a, b, preferred_element_type=F32)


def _dot_nt(a, b):
    return lax.dot_general(a, b, (((1,), (1,)), ((), ())), preferred_element_type=F32)


def _dot_tn(a, b):
    return lax.dot_general(a, b, (((0,), (0,)), ((), ())), preferred_element_type=F32)


def _proj_kernel(x_ref, pos_ref, w_ref, lbraw_ref, invf_ref, p_ref, logf_ref, *, layer):
    tm = x_ref.shape[0]
    wd = logf_ref.shape[1]
    x = x_ref[...].astype(BF16)

    def mm(j):
        return _dot(x, w_ref[:, j * wd:(j + 1) * wd])

    a = lbraw_ref[...]
    e = jnp.exp(a - jnp.max(a, axis=0, keepdims=True))
    sm = e / jnp.sum(e, axis=0, keepdims=True)
    lb = jnp.sum(sm[0:layer + 1, :], axis=0, keepdims=True)

    hq = mm(0)
    p_ref[:, 0:wd] = (hq * _sigmoid(hq)).astype(BF16)
    hf = mm(1)
    forget = lb + (1.0 - lb) * _sigmoid(hf)
    logf_ref[...] = jnp.log(forget)
    p_ref[:, wd:2 * wd] = (1.0 - forget).astype(BF16)
    p_ref[:, 2 * wd:3 * wd] = mm(2).astype(BF16)
    hg = mm(3)
    p_ref[:, 3 * wd:4 * wd] = (hg * _sigmoid(hg)).astype(BF16)

    ang = invf_ref[...] * pos_ref[...].astype(F32)
    c8 = jnp.cos(ang)
    s8 = jnp.sin(ang)
    one = jnp.ones((DF_DQK - ROT_DIM, tm), F32)
    zero = jnp.zeros((DF_DQK - ROT_DIM, tm), F32)
    z8 = jnp.zeros_like(s8)
    cos_t = jnp.concatenate([c8, c8, one, c8, c8, one], axis=0).T
    sin_lo = jnp.concatenate([-s8, z8, zero, -s8, z8, zero], axis=0).T
    sin_hi = jnp.concatenate([z8, s8, zero, z8, s8, zero], axis=0).T
    half = ROT_DIM // 2

    def rope(t):
        return (t * cos_t + pltpu.roll(t, LANES - half, 1) * sin_lo
                + pltpu.roll(t, half, 1) * sin_hi)

    q = mm(4)
    k = mm(5)
    for j in range(wd // LANES):
        sl = slice(j * LANES, (j + 1) * LANES)
        p_ref[:, 4 * wd + j * LANES:4 * wd + (j + 1) * LANES] = (
            rope(q[:, sl]) * (DF_DQK ** -0.5 * math.log2(math.e))).astype(BF16)
        p_ref[:, 5 * wd + j * LANES:5 * wd + (j + 1) * LANES] = rope(k[:, sl]).astype(BF16)
    p_ref[:, 6 * wd:7 * wd] = mm(6).astype(BF16)
    p_ref[:, 7 * wd:8 * wd] = mm(7).astype(BF16)


def _proj(x2, pos3, w_in_b, lbraw, invf8, *, tm, layer):
    n_tok, d = x2.shape
    width = w_in_b.shape[1]
    wd = lbraw.shape[1]
    return pl.pallas_call(
        functools.partial(_proj_kernel, layer=layer),
        out_shape=(jax.ShapeDtypeStruct((n_tok, width), BF16),
                   jax.ShapeDtypeStruct((n_tok, wd), F32)),
        grid=(n_tok // tm,),
        in_specs=[
            pl.BlockSpec((tm, d), lambda i: (i, 0)),
            pl.BlockSpec((None, 1, tm), lambda i: (i, 0, 0)),
            pl.BlockSpec((d, width), lambda i: (0, 0)),
            pl.BlockSpec(lbraw.shape, lambda i: (0, 0)),
            pl.BlockSpec(invf8.shape, lambda i: (0, 0)),
        ],
        out_specs=(pl.BlockSpec((tm, width), lambda i: (i, 0)),
                   pl.BlockSpec((tm, wd), lambda i: (i, 0))),
        compiler_params=pltpu.CompilerParams(
            dimension_semantics=("arbitrary",), vmem_limit_bytes=V7X_VMEM_LIMIT),
        name="proj",
    )(x2, pos3, w_in_b, lbraw, invf8)


def _hgrn_kernel(q_ref, k_ref, v_ref, g_ref, lf_ref, gain_ref, lvl_ref, tri_ref, o_ref, st_ref, *,
                 par):
    rows = q_ref.shape[0]
    c = HG_CHUNK
    dk = HG_DK

    @pl.when(pl.program_id(1) == 0)
    def _():
        st_ref[...] = jnp.zeros_like(st_ref)

    lvl = lvl_ref[...]
    tri = tri_ref[...]
    sub = lax.broadcasted_iota(jnp.int32, (c // 8, 8, dk), 1)
    row = lax.broadcasted_iota(jnp.int32, (c, dk), 0)

    def anchors(cum):
        out = []
        for m in (32, 16, 8):
            pieces = []
            for j in range(c // (2 * m)):
                a = j * 2 * m + m - 1
                pieces.append(jnp.broadcast_to(cum[a:a + 1, :], (2 * m, dk)))
            out.append(pieces[0] if len(pieces) == 1 else jnp.concatenate(pieces, axis=0))
        c8 = cum.reshape(c // 8, 8, dk)
        out.append(jnp.broadcast_to(c8[:, 3:4, :], c8.shape).reshape(c, dk))
        a2 = jnp.where(sub < 4, jnp.broadcast_to(c8[:, 1:2, :], c8.shape),
                       jnp.broadcast_to(c8[:, 5:6, :], c8.shape))
        out.append(a2.reshape(c, dk))
        return out

    def chunk_body(ci, carry):
        units = [(cc, h) for cc in range(par) for h in range(HG_HEADS)]
        r0 = [pl.multiple_of((ci * par + cc) * c, c) for cc in range(par)]
        hsl = [slice(h * dk, (h + 1) * dk) for h in range(HG_HEADS)]
        q = {u: q_ref[pl.ds(r0[u[0]], c), hsl[u[1]]].astype(F32) for u in units}
        k = {u: k_ref[pl.ds(r0[u[0]], c), hsl[u[1]]].astype(F32) for u in units}
        v = {u: v_ref[pl.ds(r0[u[0]], c), hsl[u[1]]] for u in units}
        lf = {u: lf_ref[pl.ds(r0[u[0]], c), hsl[u[1]]] for u in units}
        cum = {}
        for u in units:
            lf1 = lf[u].astype(BF16)
            r1 = lf[u] - lf1.astype(F32)
            lf2 = r1.astype(BF16)
            lf3 = (r1 - lf2.astype(F32)).astype(BF16)
            cum[u] = _dot(tri, lf1) + _dot(tri, lf2) + _dot(tri, lf3)
        operands = {}
        for u in units:
            e_lvls = [jnp.exp(-jnp.abs(cum[u] - a)) for a in anchors(cum[u])]
            e_lvls.append(jnp.where((row & 1) == 1, jnp.exp(lf[u]), 1.0))
            ops = [((q[u] * e).astype(BF16), (k[u] * e).astype(BF16)) for e in e_lvls]
            ops.append((q[u].astype(BF16), k[u].astype(BF16)))
            operands[u] = ops
        ecum = {u: jnp.exp(cum[u]) for u in units}
        k_dec = {u: (k[u] * jnp.exp(cum[u][c - 1:c, :] - cum[u])).astype(BF16) for u in units}
        q_dec = {u: (q[u] * ecum[u]).astype(BF16) for u in units}
        level_dots = {u: [_dot_nt(qe, ke) for qe, ke in operands[u]] for u in units}
        st = [st_ref[h] for h in range(HG_HEADS)]
        o_inter = {}
        for u in units:
            h = u[1]
            o_inter[u] = _dot_nt(q_dec[u], st[h].astype(BF16))
            st[h] = st[h] * ecum[u][c - 1:c, :] + _dot_tn(v[u], k_dec[u])
        for h in range(HG_HEADS):
            st_ref[h] = st[h]
        scores = {}
        for u in units:
            sc = jnp.zeros((c, c), F32)
            for i, d in enumerate(level_dots[u]):
                sc = jnp.where(lvl == i, d, sc)
            scores[u] = sc.astype(BF16)
        for u in units:
            o = o_inter[u] + _dot(scores[u], v[u])
            y = o * lax.rsqrt(jnp.mean(o * o, axis=-1, keepdims=True) + RMS_EPS)
            y = y * gain_ref[:, hsl[u[1]]] * g_ref[pl.ds(r0[u[0]], c), hsl[u[1]]].astype(F32)
            o_ref[pl.ds(r0[u[0]], c), hsl[u[1]]] = y.astype(o_ref.dtype)
        return carry

    lax.fori_loop(0, rows // (c * par), chunk_body, 0)


def _hgrn_consts():
    c = HG_CHUNK
    t = jnp.arange(c)[:, None]
    s = jnp.arange(c)[None, :]
    x = t ^ s
    lvl = jnp.full((c, c), -1, jnp.int32)
    for i, m in enumerate((32, 16, 8, 4, 2, 1)):
        lvl = jnp.where((t > s) & (x >= m) & (x < 2 * m), i, lvl)
    lvl = jnp.where(t == s, 6, lvl)
    tri = (t >= s).astype(BF16)
    return lvl, tri


def _hgrn(p3, logf3, gain, *, rows, par):
    b, s, _ = p3.shape
    wd = logf3.shape[2]
    lvl, tri = _hgrn_consts()

    def col(j):
        return pl.BlockSpec((None, rows, wd), lambda bi, si: (bi, si, j))

    return pl.pallas_call(
        functools.partial(_hgrn_kernel, par=par),
        out_shape=jax.ShapeDtypeStruct((b, s, wd), BF16),
        grid=(b, s // rows),
        in_specs=[col(0), col(1), col(2), col(3), col(0),
                  pl.BlockSpec(gain.shape, lambda bi, si: (0, 0)),
                  pl.BlockSpec(lvl.shape, lambda bi, si: (0, 0)),
                  pl.BlockSpec(tri.shape, lambda bi, si: (0, 0))],
        out_specs=col(0),
        scratch_shapes=[pltpu.VMEM((HG_HEADS, HG_DK, HG_DK), F32)],
        compiler_params=pltpu.CompilerParams(
            dimension_semantics=("arbitrary", "arbitrary"), vmem_limit_bytes=V7X_VMEM_LIMIT),
        name="hgrn",
    )(p3, p3, p3, p3, logf3, gain, lvl, tri)


def _diffattn_kernel(q_ref, k_ref, v_ref, lq1_ref, lk1_ref, lq2_ref, lk2_ref, gain_ref, o_ref,
                     m_ref, l_ref, acc_ref, *, tk, rq, lam_init):
    tq = q_ref.shape[0]
    qi = pl.program_id(2)
    lane = lax.broadcasted_iota(jnp.int32, (tq, LANES), 1)
    q = q_ref[...]
    zero = jnp.zeros_like(q)
    qm = (jnp.where(lane < DF_DQK, q, zero), jnp.where(lane >= DF_DQK, q, zero))

    m_ref[...] = jnp.full(m_ref.shape, -jnp.inf, F32)
    l_ref[...] = jnp.zeros(l_ref.shape, F32)
    acc_ref[...] = jnp.zeros(acc_ref.shape, F32)

    def tile(k0, diag):
        blocks = [(r, j) for r in range(tq // rq) for j in range(2)]
        n_keys = {r: ((r + 1) * rq if diag else tk) for r in range(tq // rq)}
        scores = {}
        for r, j in blocks:
            rows = slice(r * rq, (r + 1) * rq)
            scores[r, j] = _dot_nt(qm[j][rows], k_ref[pl.ds(k0, n_keys[r]), :])
        probs = {}
        for r, j in blocks:
            rows = slice(r * rq, (r + 1) * rq)
            s = scores[r, j]
            chunks = [s[:, c * LANES:(c + 1) * LANES] for c in range(n_keys[r] // LANES)]
            if diag:
                n_d = rq // LANES
                rr = lax.broadcasted_iota(jnp.int32, (rq, LANES), 0)
                cc = lax.broadcasted_iota(jnp.int32, (rq, LANES), 1)
                for c in range(n_d):
                    idx = len(chunks) - n_d + c
                    chunks[idx] = jnp.where(rr >= cc + c * LANES, chunks[idx], -jnp.inf)
            cmax = functools.reduce(jnp.maximum, chunks)
            m_old = m_ref[j, rows]
            m_new = jnp.maximum(m_old, jnp.max(cmax, axis=-1, keepdims=True))
            alpha = jnp.exp2(m_old - m_new)
            ps = [jnp.exp2(ch - m_new) for ch in chunks]
            l_ref[j, rows] = alpha * l_ref[j, rows] + functools.reduce(lambda a, b: a + b, ps)
            m_ref[j, rows] = m_new
            acc_ref[j, rows] = alpha * acc_ref[j, rows]
            probs[r, j] = jnp.concatenate([x.astype(BF16) for x in ps], axis=1)
        for r, j in blocks:
            rows = slice(r * rq, (r + 1) * rq)
            acc_ref[j, rows] += _dot(probs[r, j], v_ref[pl.ds(k0, n_keys[r]), :])

    def body(ki, carry):
        tile(pl.multiple_of(ki * tk, tk), False)
        return carry

    lax.fori_loop(0, (qi * tq) // tk, body, 0)
    tile(pl.multiple_of(qi * tq, tq), True)

    lam = (jnp.exp(jnp.sum(lq1_ref[...] * lk1_ref[...], keepdims=True))
           - jnp.exp(jnp.sum(lq2_ref[...] * lk2_ref[...], keepdims=True)) + lam_init)
    l0 = jnp.sum(l_ref[0], axis=-1, keepdims=True)
    l1 = jnp.sum(l_ref[1], axis=-1, keepdims=True)
    o = acc_ref[0] / l0 - lam * (acc_ref[1] / l1)
    y = o * lax.rsqrt(jnp.mean(o * o, axis=-1, keepdims=True) + RMS_EPS)
    o_ref[...] = (y * gain_ref[...] * (1.0 - lam_init)).astype(o_ref.dtype)


def _diffattn(p3, lq1, lk1, lq2, lk2, gain, *, tq, tk, rq, lam_init, col0):
    b, s, _ = p3.shape
    assert tq % tk == 0 and tq % rq == 0 and rq % LANES == 0
    small = pl.BlockSpec((1, DF_DQK), lambda bi, hi, qi: (0, 0))
    return pl.pallas_call(
        functools.partial(_diffattn_kernel, tk=tk, rq=rq, lam_init=lam_init),
        out_shape=jax.ShapeDtypeStruct((b, s, DF_HEADS * DF_DV), BF16),
        grid=(b, DF_HEADS, s // tq),
        in_specs=[
            pl.BlockSpec((None, tq, LANES), lambda bi, hi, qi: (bi, qi, col0 + hi)),
            pl.BlockSpec((None, s, LANES), lambda bi, hi, qi: (bi, 0, col0 + DF_HEADS + hi)),
            pl.BlockSpec((None, s, LANES), lambda bi, hi, qi: (bi, 0, col0 + 2 * DF_HEADS + hi)),
            small, small, small, small,
            pl.BlockSpec((1, DF_DV), lambda bi, hi, qi: (0, 0)),
        ],
        out_specs=pl.BlockSpec((None, tq, DF_DV), lambda bi, hi, qi: (bi, qi, hi)),
        scratch_shapes=[pltpu.VMEM((2, tq, LANES), F32), pltpu.VMEM((2, tq, LANES), F32),
                        pltpu.VMEM((2, tq, DF_DV), F32)],
        compiler_params=pltpu.CompilerParams(
            dimension_semantics=("arbitrary", "arbitrary", "arbitrary"),
            vmem_limit_bytes=V7X_VMEM_LIMIT),
        name="diffattn",
    )(p3, p3, p3, lq1, lk1, lq2, lk2, gain)


def _memattn_kernel(q_ref, mem_ref, w_ref, o_ref, kv_ref):
    width = MEM_HEADS * MEM_DH

    @pl.when(pl.program_id(1) == 0)
    def _():
        kv_ref[...] = _dot(mem_ref[...].astype(BF16), w_ref[...]).astype(BF16)

    for h in range(MEM_HEADS):
        hs = slice(h * MEM_DH, (h + 1) * MEM_DH)
        s = _dot_nt(q_ref[:, hs], kv_ref[:, hs]) * (MEM_DH ** -0.5)
        e = jnp.exp(s - jnp.max(s, axis=-1, keepdims=True))
        p = e / jnp.sum(e, axis=-1, keepdims=True)
        o_ref[:, hs] = _dot(p.astype(BF16), kv_ref[:, width + h * MEM_DH:width + (h + 1) * MEM_DH]
                            ).astype(o_ref.dtype)


def _memattn(p3, mem, w_kv_b, *, tq, col0):
    b, s, _ = p3.shape
    n_mem, d = mem.shape[1], mem.shape[2]
    width = MEM_HEADS * MEM_DH
    return pl.pallas_call(
        _memattn_kernel,
        out_shape=jax.ShapeDtypeStruct((b, s, width), BF16),
        grid=(b, s // tq),
        in_specs=[
            pl.BlockSpec((None, tq, width), lambda bi, qi: (bi, qi, col0)),
            pl.BlockSpec((None, n_mem, d), lambda bi, qi: (bi, 0, 0)),
            pl.BlockSpec(w_kv_b.shape, lambda bi, qi: (0, 0)),
        ],
        out_specs=pl.BlockSpec((None, tq, width), lambda bi, qi: (bi, qi, 0)),
        scratch_shapes=[pltpu.VMEM((n_mem, 2 * width), BF16)],
        compiler_params=pltpu.CompilerParams(
            dimension_semantics=("arbitrary", "arbitrary"), vmem_limit_bytes=V7X_VMEM_LIMIT),
        name="memattn",
    )(p3, mem, w_kv_b)


def _layer_norm(z, gain, bias):
    mu = jnp.mean(z, axis=-1, keepdims=True)
    zc = z - mu
    var = jnp.mean(zc * zc, axis=-1, keepdims=True)
    return zc * lax.rsqrt(var + LN_EPS) * gain + bias


def _merge_kernel(x_ref, yh_ref, yd_ref, ym_ref, wg_ref, wbh_ref, wbd_ref, wbm_ref, wo_ref,
                  g1_ref, b1_ref, wr_cat_ref, tri_ref, h_ref, meta_ref, meta_t_ref, cnt_ref, run_ref,
                  *, parts):
    tm, d = x_ref.shape
    rp = tm // parts

    @pl.when(pl.program_id(0) == 0)
    def _():
        run_ref[...] = jnp.zeros_like(run_ref)

    lane = lax.broadcasted_iota(jnp.int32, (rp, LANES), 1)
    neg = jnp.full((rp, LANES), -jnp.inf, F32)
    big = jnp.full((rp, LANES), 4 * LANES, jnp.int32)
    is_g = lane < N_GROUPS
    branches = ((yh_ref, wbh_ref), (yd_ref, wbd_ref), (ym_ref, wbm_ref))

    def project(p):
        rows = slice(p * rp, (p + 1) * rp)
        x = x_ref[rows, :]
        xb = x.astype(BF16)
        merged = None
        for j, (y_ref, wb_ref) in enumerate(branches):
            gate = _sigmoid(_dot(xb, wg_ref[:, j * d:(j + 1) * d]))
            term = gate * _dot(y_ref[rows, :], wb_ref[...])
            merged = term if merged is None else merged + term
        return x, merged.astype(BF16)

    def norm(p, x, mix):
        h = _layer_norm(DEEPNORM_ALPHA * x + mix, g1_ref[...], b1_ref[...])
        h_ref[p * rp:(p + 1) * rp, :] = h
        h_hi = h.astype(BF16)
        h_lo = (h - h_hi.astype(F32)).astype(BF16)
        return h_hi, h_lo

    def logits(h_hi, h_lo):
        t = _dot(h_hi, wr_cat_ref[...])
        return t[:, :LANES] + t[:, LANES:] + _dot(h_lo, wr_cat_ref[:, :LANES])

    def route(lg):
        g_max = jnp.max(jnp.where(is_g, lg, neg), axis=-1, keepdims=True)
        g_sum = jnp.sum(jnp.where(is_g, jnp.exp(lg - g_max), 0.0), axis=-1, keepdims=True)
        group_w = 1.0 / g_sum
        g_idx = jnp.min(jnp.where(is_g & (lg == g_max), lane, big), axis=-1, keepdims=True)
        in_grp = ((lane >= N_GROUPS) & (lane < N_GROUPS + N_EXPERTS)
                  & (jnp.right_shift(lane - N_GROUPS, 3) == g_idx))
        v1 = jnp.max(jnp.where(in_grp, lg, neg), axis=-1, keepdims=True)
        i1 = jnp.min(jnp.where(in_grp & (lg == v1), lane, big), axis=-1, keepdims=True)
        rest = in_grp & (lane != i1)
        v2 = jnp.max(jnp.where(rest, lg, neg), axis=-1, keepdims=True)
        i2 = jnp.min(jnp.where(rest & (lg == v2), lane, big), axis=-1, keepdims=True)
        e21 = jnp.exp(v2 - v1)
        w1 = group_w / (1.0 + e21)
        w2 = group_w * e21 / (1.0 + e21)
        return i1, i2, w1, w2

    def finish(p, routed, run):
        i1, i2, w1, w2 = routed
        hot1 = lane == i1
        hot2 = lane == i2
        hot = jnp.where(hot1 | hot2, 1.0, 0.0)
        before = _dot(tri_ref[...], hot.astype(BF16)) + run
        r1 = jnp.sum(jnp.where(hot1, before, 0.0), axis=-1, keepdims=True)
        r2 = jnp.sum(jnp.where(hot2, before, 0.0), axis=-1, keepdims=True)
        meta = jnp.zeros((rp, LANES), F32)
        for col, val in enumerate(((i1 - N_GROUPS).astype(F32), (i2 - N_GROUPS).astype(F32),
                                   r1, r2, w1, w2)):
            meta = jnp.where(lane == col, val, meta)
        meta_ref[p * rp:(p + 1) * rp, :] = meta
        meta_t_ref[:, p * rp:(p + 1) * rp] = meta.T[:SUBLANES, :]
        return before[rp - 1:rp, :] + hot[rp - 1:rp, :]

    run = run_ref[...]
    state = {}
    for step in range(parts + 4):
        p = step - 4
        if 0 <= p < parts:
            run = finish(p, state.pop(("routed", p)), run)
        p = step - 3
        if 0 <= p < parts:
            state["routed", p] = route(state.pop(("lg", p)))
        p = step - 2
        if 0 <= p < parts:
            state["lg", p] = logits(*state.pop(("h", p)))
        p = step - 1
        if 0 <= p < parts:
            x, merged = state.pop(("proj", p))
            state["h", p] = norm(p, x, _dot(merged, wo_ref[...]))
        p = step
        if 0 <= p < parts:
            state["proj", p] = project(p)
    run_ref[...] = run
    cnt_ref[...] = run


def _merge(x2, yh, yd, ym, wg_b, wbh_b, wbd_b, wbm_b, wo_b, g1, b1, wr_cat, *, tm, parts):
    n_tok, d = x2.shape
    rp = tm // parts
    tri = (jnp.arange(rp)[:, None] > jnp.arange(rp)[None, :]).astype(BF16)
    row = lambda w: pl.BlockSpec((tm, w), lambda i: (i, 0))
    full = lambda a: pl.BlockSpec(a.shape, lambda i: (0, 0))
    return pl.pallas_call(
        functools.partial(_merge_kernel, parts=parts),
        out_shape=(jax.ShapeDtypeStruct((n_tok, d), F32),
                   jax.ShapeDtypeStruct((n_tok, LANES), F32),
                   jax.ShapeDtypeStruct((SUBLANES, n_tok), F32),
                   jax.ShapeDtypeStruct((1, LANES), F32)),
        grid=(n_tok // tm,),
        in_specs=[row(d), row(yh.shape[1]), row(yd.shape[1]), row(ym.shape[1]),
                  full(wg_b), full(wbh_b), full(wbd_b), full(wbm_b), full(wo_b),
                  full(g1), full(b1), full(wr_cat), full(tri)],
        out_specs=(row(d), row(LANES), pl.BlockSpec((SUBLANES, tm), lambda i: (0, i)),
                   pl.BlockSpec((1, LANES), lambda i: (0, 0))),
        scratch_shapes=[pltpu.VMEM((1, LANES), F32)],
        compiler_params=pltpu.CompilerParams(
            dimension_semantics=("arbitrary",), vmem_limit_bytes=V7X_VMEM_LIMIT),
        name="merge",
    )(x2, yh, yd, ym, wg_b, wbh_b, wbd_b, wbm_b, wo_b, g1, b1, wr_cat, tri)


def _scatter_kernel(pend_ref, psize_ref, nv_ref, dest_ref, h_ref, xs_hbm, dest_smem, zero_ref,
                    row_sem, aux_sem, *, ts, rows):
    cp = pltpu.make_async_copy(dest_ref, dest_smem, aux_sem)
    cp.start()
    cp.wait()
    n_blocks = xs_hbm.shape[0] // rows

    @pl.when(pl.program_id(0) == 0)
    def _():
        zero_ref[...] = jnp.zeros_like(zero_ref)

        def fill(start):
            return pltpu.make_async_copy(zero_ref, xs_hbm.at[pl.ds(start, rows)], aux_sem)

        def fills(act):
            for e in range(N_EXPERTS):
                @pl.when(psize_ref[e] > 0)
                def _():
                    act(fill(pl.multiple_of(pend_ref[e] - rows, rows)))
            for blk in range(n_blocks - N_EXPERTS, n_blocks):
                @pl.when(blk >= nv_ref[0])
                def _():
                    act(fill(blk * rows))

        fills(lambda c: c.start())
        fills(lambda c: c.wait())

    def issue(t, carry):
        for k in range(TOP_K):
            d = dest_smem[0, 0, k * ts + t]
            pltpu.make_async_copy(h_ref.at[pl.ds(t, 1)], xs_hbm.at[pl.ds(d, 1)], row_sem).start()
        return carry

    lax.fori_loop(0, ts, issue, 0, unroll=8)
    pltpu.make_async_copy(xs_hbm.at[pl.ds(0, TOP_K * ts)], xs_hbm.at[pl.ds(0, TOP_K * ts)],
                          row_sem).wait()


def _scatter(padded_end, padded, n_valid, dest3, h, *, ts, rows, n_slots):
    n_tok, d = h.shape
    return pl.pallas_call(
        functools.partial(_scatter_kernel, ts=ts, rows=rows),
        out_shape=jax.ShapeDtypeStruct((n_slots, d), h.dtype),
        grid_spec=pltpu.PrefetchScalarGridSpec(
            num_scalar_prefetch=3,
            grid=(n_tok // ts,),
            in_specs=[pl.BlockSpec((1, 1, TOP_K * ts), lambda i, pe, ps, nv: (i, 0, 0)),
                      pl.BlockSpec((ts, d), lambda i, pe, ps, nv: (i, 0))],
            out_specs=pl.BlockSpec(memory_space=pl.ANY),
            scratch_shapes=[pltpu.SMEM((1, 1, TOP_K * ts), jnp.int32),
                            pltpu.VMEM((rows, d), h.dtype),
                            pltpu.SemaphoreType.DMA, pltpu.SemaphoreType.DMA],
        ),
        compiler_params=pltpu.CompilerParams(
            dimension_semantics=("arbitrary",), vmem_limit_bytes=V7X_VMEM_LIMIT),
        name="scatter",
    )(padded_end, padded, n_valid, dest3, h)


def _experts_kernel(be_ref, nv_ref, x_ref, wg_ref, wu_ref, wd_ref, y_ref, wg_b, wu_b, wd_b):
    j = pl.program_id(0)
    new_expert = (j == 0) | (be_ref[j] != be_ref[jnp.maximum(j - 1, 0)])

    @pl.when(new_expert & (j < nv_ref[0]))
    def _():
        wg_b[...] = wg_ref[...].astype(BF16)
        wu_b[...] = wu_ref[...].astype(BF16)
        wd_b[...] = wd_ref[...].astype(BF16)

    @pl.when(j < nv_ref[0])
    def _():
        half = x_ref.shape[0] // 2
        acts = []
        for p in range(2):
            x = x_ref[p * half:(p + 1) * half, :].astype(BF16)
            g = _dot(x, wg_b[...])
            u = _dot(x, wu_b[...])
            acts.append((g * _sigmoid(g) * u).astype(BF16))
        for p in range(2):
            y_ref[p * half:(p + 1) * half, :] = _dot(acts[p], wd_b[...])

    @pl.when(j >= nv_ref[0])
    def _():
        y_ref[...] = jnp.zeros_like(y_ref)


def _experts(block_expert, n_valid, xs, w_gate, w_up, w_down, *, rows):
    n_slots, d = xs.shape
    de = w_gate.shape[2]
    return pl.pallas_call(
        _experts_kernel,
        out_shape=jax.ShapeDtypeStruct((n_slots, d), F32),
        grid_spec=pltpu.PrefetchScalarGridSpec(
            num_scalar_prefetch=2,
            grid=(n_slots // rows,),
            in_specs=[pl.BlockSpec((rows, d), lambda j, be, nv: (jnp.minimum(j, nv[0] - 1), 0)),
                      pl.BlockSpec((None, d, de), lambda j, be, nv: (be[j], 0, 0)),
                      pl.BlockSpec((None, d, de), lambda j, be, nv: (be[j], 0, 0)),
                      pl.BlockSpec((None, de, d), lambda j, be, nv: (be[j], 0, 0))],
            out_specs=pl.BlockSpec((rows, d), lambda j, be, nv: (j, 0)),
            scratch_shapes=[pltpu.VMEM((d, de), BF16), pltpu.VMEM((d, de), BF16),
                            pltpu.VMEM((de, d), BF16)],
        ),
        compiler_params=pltpu.CompilerParams(
            dimension_semantics=("arbitrary",), vmem_limit_bytes=V7X_VMEM_LIMIT),
        name="experts",
    )(block_expert, n_valid, xs, w_gate, w_up, w_down)


def _combine_kernel(dest_ref, dest_next_ref, h_ref, meta_ref, g2_ref, b2_ref, y_hbm, o_ref,
                    dest_smem, ybuf, row_sem, idx_sem, *, tc):
    i = pl.program_id(0)
    slot = i % 2

    def start_gathers(idx_ref, s):
        cp = pltpu.make_async_copy(idx_ref, dest_smem.at[s], idx_sem)
        cp.start()
        cp.wait()

        def issue(t, carry):
            for k in range(TOP_K):
                d = dest_smem[s, 0, 0, k * tc + t]
                pltpu.make_async_copy(y_hbm.at[pl.ds(d, 1)], ybuf.at[s, k, pl.ds(t, 1)],
                                      row_sem.at[s]).start()
            return carry

        lax.fori_loop(0, tc, issue, 0, unroll=8)

    @pl.when(i == 0)
    def _():
        start_gathers(dest_ref, 0)

    for s in range(2):
        @pl.when((i + 1 < pl.num_programs(0)) & (slot == 1 - s))
        def _():
            start_gathers(dest_next_ref, s)

    for k in range(TOP_K):
        pltpu.make_async_copy(y_hbm.at[pl.ds(0, tc)], ybuf.at[slot, k], row_sem.at[slot]).wait()

    meta = meta_ref[...]
    ffn = meta[:, 4:5] * ybuf[slot, 0] + meta[:, 5:6] * ybuf[slot, 1]
    o_ref[...] = _layer_norm(DEEPNORM_ALPHA * h_ref[...] + ffn, g2_ref[...], b2_ref[...])


def _combine(dest3, h, meta, g2, b2, y, *, tc):
    n_tok, d = h.shape
    n_steps = n_tok // tc
    return pl.pallas_call(
        functools.partial(_combine_kernel, tc=tc),
        out_shape=jax.ShapeDtypeStruct((n_tok, d), F32),
        grid=(n_steps,),
        in_specs=[pl.BlockSpec((1, 1, TOP_K * tc), lambda i: (i, 0, 0)),
                  pl.BlockSpec((1, 1, TOP_K * tc),
                               lambda i: (jnp.minimum(i + 1, n_steps - 1), 0, 0)),
                  pl.BlockSpec((tc, d), lambda i: (i, 0)),
                  pl.BlockSpec((tc, LANES), lambda i: (i, 0)),
                  pl.BlockSpec(g2.shape, lambda i: (0, 0)),
                  pl.BlockSpec(b2.shape, lambda i: (0, 0)),
                  pl.BlockSpec(memory_space=pl.ANY)],
        out_specs=pl.BlockSpec((tc, d), lambda i: (i, 0)),
        scratch_shapes=[pltpu.SMEM((2, 1, 1, TOP_K * tc), jnp.int32),
                        pltpu.VMEM((2, TOP_K, tc, d), F32),
                        pltpu.SemaphoreType.DMA((2,)),
                        pltpu.SemaphoreType.DMA],
        compiler_params=pltpu.CompilerParams(
            dimension_semantics=("arbitrary",), vmem_limit_bytes=V7X_VMEM_LIMIT),
        name="combine",
    )(dest3, dest3, h, meta, g2, b2, y)


def _w_in_column_order():
    hgw = HG_HEADS * HG_DK
    qkw = DF_HEADS * DF_DQK
    base = 4 * hgw
    cols = list(range(base))
    for first in (base, base + 2 * qkw):
        for h in range(DF_HEADS):
            cols += list(range(first + h * DF_DQK, first + (h + 1) * DF_DQK))
            cols += list(range(first + qkw + h * DF_DQK, first + qkw + (h + 1) * DF_DQK))
    total = base + 4 * qkw + DF_HEADS * DF_DV + MEM_HEADS * MEM_DH
    cols += list(range(base + 4 * qkw, total))
    return jnp.asarray(cols, jnp.int32)


def kernel(x, mem, positions, w_in, w_gates, hgrn_lower_bounds, hgrn_norm_gain, diff_lambda_q1, diff_lambda_k1, diff_lambda_q2, diff_lambda_k2, diff_subln_gain, w_mem_kv, w_branch_hgrn, w_branch_diff, w_branch_mem, w_out, ln1_gain, ln1_bias, w_group_router, w_expert_router, w_expert_gate, w_expert_up, w_expert_down, ln2_gain, ln2_bias):
    b, s, d = x.shape
    n_tok = b * s
    t = _tiles(s, n_tok)
    assert w_in.shape[0] == DEPTH
    layer = 0
    lam_init = 0.8 - 0.6 * math.exp(-0.3 * layer)
    hgw = HG_HEADS * HG_DK

    w_in_b = jnp.take(w_in[layer], _w_in_column_order(), axis=1).astype(BF16)
    inv_freq = ROPE_THETA ** (-jnp.arange(0, ROT_DIM, 2, dtype=F32) / ROT_DIM)
    invf8 = inv_freq.reshape(ROT_DIM // 2, 1)
    w_router = jnp.concatenate(
        [w_group_router[layer], w_expert_router[layer],
         jnp.zeros((d, LANES - N_GROUPS - N_EXPERTS), F32)], axis=1)
    wr_hi = w_router.astype(BF16)
    wr_lo = (w_router - wr_hi.astype(F32)).astype(BF16)
    wr_cat = jnp.concatenate([wr_hi, wr_lo], axis=1)

    x2 = x.reshape(n_tok, d)
    pos3 = positions.reshape(n_tok // t["proj_rows"], 1, t["proj_rows"])
    p, logf = _proj(x2, pos3, w_in_b, hgrn_lower_bounds, invf8, tm=t["proj_rows"], layer=layer)
    p3 = p.reshape(b, s, p.shape[1])

    y_hg = _hgrn(p3, logf.reshape(b, s, hgw), hgrn_norm_gain[layer].reshape(1, hgw),
                 rows=t["hgrn_rows"], par=t["hgrn_par"])
    y_df = _diffattn(p3, diff_lambda_q1[layer].reshape(1, -1), diff_lambda_k1[layer].reshape(1, -1),
                     diff_lambda_q2[layer].reshape(1, -1), diff_lambda_k2[layer].reshape(1, -1),
                     diff_subln_gain[layer].reshape(1, -1), tq=t["attn_q"], tk=t["attn_k"], rq=t["attn_rows"],
                     lam_init=lam_init, col0=4 * hgw // LANES)
    y_mem = _memattn(p3, mem, w_mem_kv[layer].astype(BF16), tq=t["mem_q"],
                     col0=7 * hgw // (MEM_HEADS * MEM_DH))

    h, meta, meta_t, counts_f = _merge(
        x2, y_hg.reshape(n_tok, -1), y_df.reshape(n_tok, -1), y_mem.reshape(n_tok, -1),
        w_gates[layer].astype(BF16), w_branch_hgrn[layer].astype(BF16),
        w_branch_diff[layer].astype(BF16), w_branch_mem[layer].astype(BF16),
        w_out[layer].astype(BF16), ln1_gain[layer].reshape(1, d), ln1_bias[layer].reshape(1, d),
        wr_cat, tm=t["merge_rows"], parts=t["merge_parts"])

    rows = t["moe_rows"]
    counts = counts_f[0, N_GROUPS:N_GROUPS + N_EXPERTS].astype(jnp.int32)
    padded = (counts + rows - 1) // rows * rows
    padded_end = jnp.cumsum(padded)
    padded_start = padded_end - padded
    expert_id = meta_t[0:TOP_K].astype(jnp.int32)
    rank = meta_t[TOP_K:2 * TOP_K].astype(jnp.int32)
    dest = rank
    for e in range(N_EXPERTS):
        dest = dest + jnp.where(expert_id == e, padded_start[e], 0)

    def dest_tiles(tile):
        return dest.reshape(TOP_K, n_tok // tile, tile).transpose(1, 0, 2).reshape(
            n_tok // tile, 1, TOP_K * tile)
    n_slots = n_tok * TOP_K + N_EXPERTS * rows
    n_blocks = n_slots // rows
    block_start = jnp.arange(n_blocks, dtype=jnp.int32) * rows
    block_expert = jnp.minimum(
        jnp.sum((padded_end[None, :] <= block_start[:, None]).astype(jnp.int32), axis=1),
        N_EXPERTS - 1)
    n_valid = (padded_end[-1:] // rows).astype(jnp.int32)

    ts = t["scatter_rows"]
    xs = _scatter(padded_end.astype(jnp.int32), padded, n_valid,
                  dest_tiles(ts), h, ts=ts, rows=rows, n_slots=n_slots)
    y = _experts(block_expert, n_valid, xs, w_expert_gate[layer],
                 w_expert_up[layer], w_expert_down[layer], rows=rows)
    tc = t["combine_rows"]
    out = _combine(dest_tiles(tc), h, meta,
                   ln2_gain[layer].reshape(1, d), ln2_bias[layer].reshape(1, d), y, tc=tc)
    return out.reshape(b, s, d)
```

```python
import functools
import math

import jax
import jax.numpy as jnp
from jax import lax
from jax.experimental import pallas as pl
from jax.experimental.pallas import tpu as pltpu

F32 = jnp.float32
BF16 = jnp.bfloat16

HG_HEADS = 4
HG_DK = 128
HG_CHUNK = 64
DF_HEADS = 4
DF_DQK = 64
DF_DV = 128
ROPE_THETA = 500000.0
ROT_DIM = 16
MEM_HEADS = 4
MEM_DH = 128
N_BRANCH = 3
N_GROUPS = 4
EXPERTS_PER_GROUP = 8
N_EXPERTS = 32
TOP_K = 2
DEPTH = 1
DEEPNORM_ALPHA = (2.0 * DEPTH) ** 0.25
LN_EPS = 1e-5
RMS_EPS = 1e-6
LANES = 128
SUBLANES = 8
V7X_VMEM_LIMIT = 56 * 1024 * 1024


def _tiles(seq, n_tok):
    return dict(
        proj_rows=min(1024, n_tok),
        hgrn_rows=min(512, seq),
        hgrn_par=4,
        attn_q=min(2048, seq),
        attn_k=min(512, seq),
        attn_rows=256,
        mem_q=min(512, seq),
        merge_rows=min(1024, n_tok),
        merge_parts=4,
        scatter_rows=min(1024, n_tok),
        moe_rows=512,
        combine_rows=min(512, n_tok),
    )


def _sigmoid(v):
    return 1.0 / (1.0 + jnp.exp(-v))


def _dot(a, b):
    return jnp.dot(a, b, preferred_element_type=F32)


def _dot_nt(a, b):
    return lax.dot_general(a, b, (((1,), (1,)), ((), ())), preferred_element_type=F32)


def _dot_tn(a, b):
    return lax.dot_general(a, b, (((0,), (0,)), ((), ())), preferred_element_type=F32)


def _proj_kernel(x_ref, pos_ref, w_ref, lbraw_ref, invf_ref, p_ref, logf_ref, *, layer):
    tm = x_ref.shape[0]
    wd = logf_ref.shape[1]
    x = x_ref[...].astype(BF16)

    def mm(j):
        return _dot(x, w_ref[:, j * wd:(j + 1) * wd])

    a = lbraw_ref[...]
    e = jnp.exp(a - jnp.max(a, axis=0, keepdims=True))
    sm = e / jnp.sum(e, axis=0, keepdims=True)
    lb = jnp.sum(sm[0:layer + 1, :], axis=0, keepdims=True)

    hq = mm(0)
    p_ref[:, 0:wd] = (hq * _sigmoid(hq)).astype(BF16)
    hf = mm(1)
    forget = lb + (1.0 - lb) * _sigmoid(hf)
    logf_ref[...] = jnp.log(forget)
    p_ref[:, wd:2 * wd] = (1.0 - forget).astype(BF16)
    p_ref[:, 2 * wd:3 * wd] = mm(2).astype(BF16)
    hg = mm(3)
    p_ref[:, 3 * wd:4 * wd] = (hg * _sigmoid(hg)).astype(BF16)

    ang = invf_ref[...] * pos_ref[...].astype(F32)
    c8 = jnp.cos(ang)
    s8 = jnp.sin(ang)
    one = jnp.ones((DF_DQK - ROT_DIM, tm), F32)
    zero = jnp.zeros((DF_DQK - ROT_DIM, tm), F32)
    z8 = jnp.zeros_like(s8)
    cos_t = jnp.concatenate([c8, c8, one, c8, c8, one], axis=0).T
    sin_lo = jnp.concatenate([-s8, z8, zero, -s8, z8, zero], axis=0).T
    sin_hi = jnp.concatenate([z8, s8, zero, z8, s8, zero], axis=0).T
    half = ROT_DIM // 2

    def rope(t):
        return (t * cos_t + pltpu.roll(t, LANES - half, 1) * sin_lo
                + pltpu.roll(t, half, 1) * sin_hi)

    q = mm(4)
    k = mm(5)
    for j in range(wd // LANES):
        sl = slice(j * LANES, (j + 1) * LANES)
        p_ref[:, 4 * wd + j * LANES:4 * wd + (j + 1) * LANES] = (
            rope(q[:, sl]) * (DF_DQK ** -0.5 * math.log2(math.e))).astype(BF16)
        p_ref[:, 5 * wd + j * LANES:5 * wd + (j + 1) * LANES] = rope(k[:, sl]).astype(BF16)
    p_ref[:, 6 * wd:7 * wd] = mm(6).astype(BF16)
    p_ref[:, 7 * wd:8 * wd] = mm(7).astype(BF16)


def _proj(x2, pos3, w_in_b, lbraw, invf8, *, tm, layer):
    n_tok, d = x2.shape
    width = w_in_b.shape[1]
    wd = lbraw.shape[1]
    return pl.pallas_call(
        functools.partial(_proj_kernel, layer=layer),
        out_shape=(jax.ShapeDtypeStruct((n_tok, width), BF16),
                   jax.ShapeDtypeStruct((n_tok, wd), F32)),
        grid=(n_tok // tm,),
        in_specs=[
            pl.BlockSpec((tm, d), lambda i: (i, 0)),
            pl.BlockSpec((None, 1, tm), lambda i: (i, 0, 0)),
            pl.BlockSpec((d, width), lambda i: (0, 0)),
            pl.BlockSpec(lbraw.shape, lambda i: (0, 0)),
            pl.BlockSpec(invf8.shape, lambda i: (0, 0)),
        ],
        out_specs=(pl.BlockSpec((tm, width), lambda i: (i, 0)),
                   pl.BlockSpec((tm, wd), lambda i: (i, 0))),
        compiler_params=pltpu.CompilerParams(
            dimension_semantics=("arbitrary",), vmem_limit_bytes=V7X_VMEM_LIMIT),
        name="proj",
    )(x2, pos3, w_in_b, lbraw, invf8)


def _hgrn_kernel(q_ref, k_ref, v_ref, g_ref, lf_ref, gain_ref, lvl_ref, tri_ref, o_ref, st_ref, *,
                 par):
    rows = q_ref.shape[0]
    c = HG_CHUNK
    dk = HG_DK

    @pl.when(pl.program_id(1) == 0)
    def _():
        st_ref[...] = jnp.zeros_like(st_ref)

    lvl = lvl_ref[...]
    tri = tri_ref[...]
    sub = lax.broadcasted_iota(jnp.int32, (c // 8, 8, dk), 1)
    row = lax.broadcasted_iota(jnp.int32, (c, dk), 0)

    def anchors(cum):
        out = []
        for m in (32, 16, 8):
            pieces = []
            for j in range(c // (2 * m)):
                a = j * 2 * m + m - 1
                pieces.append(jnp.broadcast_to(cum[a:a + 1, :], (2 * m, dk)))
            out.append(pieces[0] if len(pieces) == 1 else jnp.concatenate(pieces, axis=0))
        c8 = cum.reshape(c // 8, 8, dk)
        out.append(jnp.broadcast_to(c8[:, 3:4, :], c8.shape).reshape(c, dk))
        a2 = jnp.where(sub < 4, jnp.broadcast_to(c8[:, 1:2, :], c8.shape),
                       jnp.broadcast_to(c8[:, 5:6, :], c8.shape))
        out.append(a2.reshape(c, dk))
        return out

    def chunk_body(ci, carry):
        units = [(cc, h) for cc in range(par) for h in range(HG_HEADS)]
        r0 = [pl.multiple_of((ci * par + cc) * c, c) for cc in range(par)]
        hsl = [slice(h * dk, (h + 1) * dk) for h in range(HG_HEADS)]
        q = {u: q_ref[pl.ds(r0[u[0]], c), hsl[u[1]]].astype(F32) for u in units}
        k = {u: k_ref[pl.ds(r0[u[0]], c), hsl[u[1]]].astype(F32) for u in units}
        v = {u: v_ref[pl.ds(r0[u[0]], c), hsl[u[1]]] for u in units}
        lf = {u: lf_ref[pl.ds(r0[u[0]], c), hsl[u[1]]] for u in units}
        cum = {}
        for u in units:
            lf1 = lf[u].astype(BF16)
            r1 = lf[u] - lf1.astype(F32)
            lf2 = r1.astype(BF16)
            lf3 = (r1 - lf2.astype(F32)).astype(BF16)
            cum[u] = _dot(tri, lf1) + _dot(tri, lf2) + _dot(tri, lf3)
        operands = {}
        for u in units:
            e_lvls = [jnp.exp(-jnp.abs(cum[u] - a)) for a in anchors(cum[u])]
            e_lvls.append(jnp.where((row & 1) == 1, jnp.exp(lf[u]), 1.0))
            ops = [((q[u] * e).astype(BF16), (k[u] * e).astype(BF16)) for e in e_lvls]
            ops.append((q[u].astype(BF16), k[u].astype(BF16)))
            operands[u] = ops
        ecum = {u: jnp.exp(cum[u]) for u in units}
        k_dec = {u: (k[u] * jnp.exp(cum[u][c - 1:c, :] - cum[u])).astype(BF16) for u in units}
        q_dec = {u: (q[u] * ecum[u]).astype(BF16) for u in units}
        level_dots = {u: [_dot_nt(qe, ke) for qe, ke in operands[u]] for u in units}
        st = [st_ref[h] for h in range(HG_HEADS)]
        o_inter = {}
        for u in units:
            h = u[1]
            o_inter[u] = _dot_nt(q_dec[u], st[h].astype(BF16))
            st[h] = st[h] * ecum[u][c - 1:c, :] + _dot_tn(v[u], k_dec[u])
        for h in range(HG_HEADS):
            st_ref[h] = st[h]
        scores = {}
        for u in units:
            sc = jnp.zeros((c, c), F32)
            for i, d in enumerate(level_dots[u]):
                sc = jnp.where(lvl == i, d, sc)
            scores[u] = sc.astype(BF16)
        for u in units:
            o = o_inter[u] + _dot(scores[u], v[u])
            y = o * lax.rsqrt(jnp.mean(o * o, axis=-1, keepdims=True) + RMS_EPS)
            y = y * gain_ref[:, hsl[u[1]]] * g_ref[pl.ds(r0[u[0]], c), hsl[u[1]]].astype(F32)
            o_ref[pl.ds(r0[u[0]], c), hsl[u[1]]] = y.astype(o_ref.dtype)
        return carry

    lax.fori_loop(0, rows // (c * par), chunk_body, 0)


def _hgrn_consts():
    c = HG_CHUNK
    t = jnp.arange(c)[:, None]
    s = jnp.arange(c)[None, :]
    x = t ^ s
    lvl = jnp.full((c, c), -1, jnp.int32)
    for i, m in enumerate((32, 16, 8, 4, 2, 1)):
        lvl = jnp.where((t > s) & (x >= m) & (x < 2 * m), i, lvl)
    lvl = jnp.where(t == s, 6, lvl)
    tri = (t >= s).astype(BF16)
    return lvl, tri


def _hgrn(p3, logf3, gain, *, rows, par):
    b, s, _ = p3.shape
    wd = logf3.shape[2]
    lvl, tri = _hgrn_consts()

    def col(j):
        return pl.BlockSpec((None, rows, wd), lambda bi, si: (bi, si, j))

    return pl.pallas_call(
        functools.partial(_hgrn_kernel, par=par),
        out_shape=jax.ShapeDtypeStruct((b, s, wd), BF16),
        grid=(b, s // rows),
        in_specs=[col(0), col(1), col(2), col(3), col(0),
                  pl.BlockSpec(gain.shape, lambda bi, si: (0, 0)),
                  pl.BlockSpec(lvl.shape, lambda bi, si: (0, 0)),
                  pl.BlockSpec(tri.shape, lambda bi, si: (0, 0))],
        out_specs=col(0),
        scratch_shapes=[pltpu.VMEM((HG_HEADS, HG_DK, HG_DK), F32)],
        compiler_params=pltpu.CompilerParams(
            dimension_semantics=("arbitrary", "arbitrary"), vmem_limit_bytes=V7X_VMEM_LIMIT),
        name="hgrn",
    )(p3, p3, p3, p3, logf3, gain, lvl, tri)


def _diffattn_kernel(q_ref, k_ref, v_ref, lq1_ref, lk1_ref, lq2_ref, lk2_ref, gain_ref, o_ref,
                     m_ref, l_ref, acc_ref, *, tk, rq, lam_init):
    tq = q_ref.shape[0]
    qi = pl.program_id(2)
    lane = lax.broadcasted_iota(jnp.int32, (tq, LANES), 1)
    q = q_ref[...]
    zero = jnp.zeros_like(q)
    qm = (jnp.where(lane < DF_DQK, q, zero), jnp.where(lane >= DF_DQK, q, zero))

    m_ref[...] = jnp.full(m_ref.shape, -jnp.inf, F32)
    l_ref[...] = jnp.zeros(l_ref.shape, F32)
    acc_ref[...] = jnp.zeros(acc_ref.shape, F32)

    def tile(k0, diag):
        rb = LANES if diag else rq
        blocks = [(r, j) for r in range(tq // rb) for j in range(2)]
        n_keys = {r: ((r + 1) * rb if diag else tk) for r in range(tq // rb)}
        scores = {}
        for r, j in blocks:
            rows = slice(r * rb, (r + 1) * rb)
            scores[r, j] = _dot_nt(qm[j][rows], k_ref[pl.ds(k0, n_keys[r]), :])
        probs = {}
        for r, j in blocks:
            rows = slice(r * rb, (r + 1) * rb)
            s = scores[r, j]
            chunks = [s[:, c * LANES:(c + 1) * LANES] for c in range(n_keys[r] // LANES)]
            if diag:
                n_d = rb // LANES
                rr = lax.broadcasted_iota(jnp.int32, (rb, LANES), 0)
                cc = lax.broadcasted_iota(jnp.int32, (rb, LANES), 1)
                for c in range(n_d):
                    idx = len(chunks) - n_d + c
                    chunks[idx] = jnp.where(rr >= cc + c * LANES, chunks[idx], -jnp.inf)
            cmax = functools.reduce(jnp.maximum, chunks)
            m_old = m_ref[j, rows]
            m_new = jnp.maximum(m_old, jnp.max(cmax, axis=-1, keepdims=True))
            alpha = jnp.exp2(m_old - m_new)
            ps = [jnp.exp2(ch - m_new) for ch in chunks]
            l_ref[j, rows] = alpha * l_ref[j, rows] + functools.reduce(lambda a, b: a + b, ps)
            m_ref[j, rows] = m_new
            acc_ref[j, rows] = alpha * acc_ref[j, rows]
            probs[r, j] = jnp.concatenate([x.astype(BF16) for x in ps], axis=1)
        for r, j in blocks:
            rows = slice(r * rb, (r + 1) * rb)
            acc_ref[j, rows] += _dot(probs[r, j], v_ref[pl.ds(k0, n_keys[r]), :])

    def body(ki, carry):
        tile(pl.multiple_of(ki * tk, tk), False)
        return carry

    lax.fori_loop(0, (qi * tq) // tk, body, 0)
    tile(pl.multiple_of(qi * tq, tq), True)

    lam = (jnp.exp(jnp.sum(lq1_ref[...] * lk1_ref[...], keepdims=True))
           - jnp.exp(jnp.sum(lq2_ref[...] * lk2_ref[...], keepdims=True)) + lam_init)
    l0 = jnp.sum(l_ref[0], axis=-1, keepdims=True)
    l1 = jnp.sum(l_ref[1], axis=-1, keepdims=True)
    o = acc_ref[0] / l0 - lam * (acc_ref[1] / l1)
    y = o * lax.rsqrt(jnp.mean(o * o, axis=-1, keepdims=True) + RMS_EPS)
    o_ref[...] = (y * gain_ref[...] * (1.0 - lam_init)).astype(o_ref.dtype)


def _diffattn(p3, lq1, lk1, lq2, lk2, gain, *, tq, tk, rq, lam_init, col0):
    b, s, _ = p3.shape
    assert tq % tk == 0 and tq % rq == 0 and rq % LANES == 0
    small = pl.BlockSpec((1, DF_DQK), lambda bi, hi, qi: (0, 0))
    return pl.pallas_call(
        functools.partial(_diffattn_kernel, tk=tk, rq=rq, lam_init=lam_init),
        out_shape=jax.ShapeDtypeStruct((b, s, DF_HEADS * DF_DV), BF16),
        grid=(b, DF_HEADS, s // tq),
        in_specs=[
            pl.BlockSpec((None, tq, LANES), lambda bi, hi, qi: (bi, qi, col0 + hi)),
            pl.BlockSpec((None, s, LANES), lambda bi, hi, qi: (bi, 0, col0 + DF_HEADS + hi)),
            pl.BlockSpec((None, s, LANES), lambda bi, hi, qi: (bi, 0, col0 + 2 * DF_HEADS + hi)),
            small, small, small, small,
            pl.BlockSpec((1, DF_DV), lambda bi, hi, qi: (0, 0)),
        ],
        out_specs=pl.BlockSpec((None, tq, DF_DV), lambda bi, hi, qi: (bi, qi, hi)),
        scratch_shapes=[pltpu.VMEM((2, tq, LANES), F32), pltpu.VMEM((2, tq, LANES), F32),
                        pltpu.VMEM((2, tq, DF_DV), F32)],
        compiler_params=pltpu.CompilerParams(
            dimension_semantics=("arbitrary", "arbitrary", "arbitrary"),
            vmem_limit_bytes=V7X_VMEM_LIMIT),
        name="diffattn",
    )(p3, p3, p3, lq1, lk1, lq2, lk2, gain)


def _memattn_kernel(q_ref, mem_ref, w_ref, o_ref, kv_ref):
    width = MEM_HEADS * MEM_DH

    @pl.when(pl.program_id(1) == 0)
    def _():
        kv_ref[...] = _dot(mem_ref[...].astype(BF16), w_ref[...]).astype(BF16)

    for h in range(MEM_HEADS):
        hs = slice(h * MEM_DH, (h + 1) * MEM_DH)
        s = _dot_nt(q_ref[:, hs], kv_ref[:, hs]) * (MEM_DH ** -0.5)
        e = jnp.exp(s - jnp.max(s, axis=-1, keepdims=True))
        p = e / jnp.sum(e, axis=-1, keepdims=True)
        o_ref[:, hs] = _dot(p.astype(BF16), kv_ref[:, width + h * MEM_DH:width + (h + 1) * MEM_DH]
                            ).astype(o_ref.dtype)


def _memattn(p3, mem, w_kv_b, *, tq, col0):
    b, s, _ = p3.shape
    n_mem, d = mem.shape[1], mem.shape[2]
    width = MEM_HEADS * MEM_DH
    return pl.pallas_call(
        _memattn_kernel,
        out_shape=jax.ShapeDtypeStruct((b, s, width), BF16),
        grid=(b, s // tq),
        in_specs=[
            pl.BlockSpec((None, tq, width), lambda bi, qi: (bi, qi, col0)),
            pl.BlockSpec((None, n_mem, d), lambda bi, qi: (bi, 0, 0)),
            pl.BlockSpec(w_kv_b.shape, lambda bi, qi: (0, 0)),
        ],
        out_specs=pl.BlockSpec((None, tq, width), lambda bi, qi: (bi, qi, 0)),
        scratch_shapes=[pltpu.VMEM((n_mem, 2 * width), BF16)],
        compiler_params=pltpu.CompilerParams(
            dimension_semantics=("arbitrary", "arbitrary"), vmem_limit_bytes=V7X_VMEM_LIMIT),
        name="memattn",
    )(p3, mem, w_kv_b)


def _layer_norm(z, gain, bias):
    mu = jnp.mean(z, axis=-1, keepdims=True)
    zc = z - mu
    var = jnp.mean(zc * zc, axis=-1, keepdims=True)
    return zc * lax.rsqrt(var + LN_EPS) * gain + bias


def _merge_kernel(x_ref, yh_ref, yd_ref, ym_ref, wg_ref, wbh_ref, wbd_ref, wbm_ref, wo_ref,
                  g1_ref, b1_ref, wr_cat_ref, tri_ref, h_ref, meta_ref, meta_t_ref, cnt_ref, run_ref,
                  *, parts):
    tm, d = x_ref.shape
    rp = tm // parts

    @pl.when(pl.program_id(0) == 0)
    def _():
        run_ref[...] = jnp.zeros_like(run_ref)

    lane = lax.broadcasted_iota(jnp.int32, (rp, LANES), 1)
    neg = jnp.full((rp, LANES), -jnp.inf, F32)
    big = jnp.full((rp, LANES), 4 * LANES, jnp.int32)
    is_g = lane < N_GROUPS
    branches = ((yh_ref, wbh_ref), (yd_ref, wbd_ref), (ym_ref, wbm_ref))

    def project(p):
        rows = slice(p * rp, (p + 1) * rp)
        x = x_ref[rows, :]
        xb = x.astype(BF16)
        merged = None
        for j, (y_ref, wb_ref) in enumerate(branches):
            gate = _sigmoid(_dot(xb, wg_ref[:, j * d:(j + 1) * d]))
            term = gate * _dot(y_ref[rows, :], wb_ref[...])
            merged = term if merged is None else merged + term
        return x, merged.astype(BF16)

    def norm(p, x, mix):
        h = _layer_norm(DEEPNORM_ALPHA * x + mix, g1_ref[...], b1_ref[...])
        h_ref[p * rp:(p + 1) * rp, :] = h
        h_hi = h.astype(BF16)
        h_lo = (h - h_hi.astype(F32)).astype(BF16)
        return h_hi, h_lo

    def logits(h_hi, h_lo):
        t = _dot(h_hi, wr_cat_ref[...])
        return t[:, :LANES] + t[:, LANES:] + _dot(h_lo, wr_cat_ref[:, :LANES])

    def route(lg):
        g_max = jnp.max(jnp.where(is_g, lg, neg), axis=-1, keepdims=True)
        g_sum = jnp.sum(jnp.where(is_g, jnp.exp(lg - g_max), 0.0), axis=-1, keepdims=True)
        group_w = 1.0 / g_sum
        g_idx = jnp.min(jnp.where(is_g & (lg == g_max), lane, big), axis=-1, keepdims=True)
        in_grp = ((lane >= N_GROUPS) & (lane < N_GROUPS + N_EXPERTS)
                  & (jnp.right_shift(lane - N_GROUPS, 3) == g_idx))
        v1 = jnp.max(jnp.where(in_grp, lg, neg), axis=-1, keepdims=True)
        i1 = jnp.min(jnp.where(in_grp & (lg == v1), lane, big), axis=-1, keepdims=True)
        rest = in_grp & (lane != i1)
        v2 = jnp.max(jnp.where(rest, lg, neg), axis=-1, keepdims=True)
        i2 = jnp.min(jnp.where(rest & (lg == v2), lane, big), axis=-1, keepdims=True)
        e21 = jnp.exp(v2 - v1)
        w1 = group_w / (1.0 + e21)
        w2 = group_w * e21 / (1.0 + e21)
        return i1, i2, w1, w2

    def finish(p, routed, run):
        i1, i2, w1, w2 = routed
        hot1 = lane == i1
        hot2 = lane == i2
        hot = jnp.where(hot1 | hot2, 1.0, 0.0)
        before = _dot(tri_ref[...], hot.astype(BF16)) + run
        r1 = jnp.sum(jnp.where(hot1, before, 0.0), axis=-1, keepdims=True)
        r2 = jnp.sum(jnp.where(hot2, before, 0.0), axis=-1, keepdims=True)
        meta = jnp.zeros((rp, LANES), F32)
        for col, val in enumerate(((i1 - N_GROUPS).astype(F32), (i2 - N_GROUPS).astype(F32),
                                   r1, r2, w1, w2)):
            meta = jnp.where(lane == col, val, meta)
        meta_ref[p * rp:(p + 1) * rp, :] = meta
        meta_t_ref[:, p * rp:(p + 1) * rp] = meta.T[:SUBLANES, :]
        return before[rp - 1:rp, :] + hot[rp - 1:rp, :]

    run = run_ref[...]
    state = {}
    for step in range(parts + 4):
        p = step - 4
        if 0 <= p < parts:
            run = finish(p, state.pop(("routed", p)), run)
        p = step - 3
        if 0 <= p < parts:
            state["routed", p] = route(state.pop(("lg", p)))
        p = step - 2
        if 0 <= p < parts:
            state["lg", p] = logits(*state.pop(("h", p)))
        p = step - 1
        if 0 <= p < parts:
            x, merged = state.pop(("proj", p))
            state["h", p] = norm(p, x, _dot(merged, wo_ref[...]))
        p = step
        if 0 <= p < parts:
            state["proj", p] = project(p)
    run_ref[...] = run
    cnt_ref[...] = run


def _merge(x2, yh, yd, ym, wg_b, wbh_b, wbd_b, wbm_b, wo_b, g1, b1, wr_cat, *, tm, parts):
    n_tok, d = x2.shape
    rp = tm // parts
    tri = (jnp.arange(rp)[:, None] > jnp.arange(rp)[None, :]).astype(BF16)
    row = lambda w: pl.BlockSpec((tm, w), lambda i: (i, 0))
    full = lambda a: pl.BlockSpec(a.shape, lambda i: (0, 0))
    return pl.pallas_call(
        functools.partial(_merge_kernel, parts=parts),
        out_shape=(jax.ShapeDtypeStruct((n_tok, d), F32),
                   jax.ShapeDtypeStruct((n_tok, LANES), F32),
                   jax.ShapeDtypeStruct((SUBLANES, n_tok), F32),
                   jax.ShapeDtypeStruct((1, LANES), F32)),
        grid=(n_tok // tm,),
        in_specs=[row(d), row(yh.shape[1]), row(yd.shape[1]), row(ym.shape[1]),
                  full(wg_b), full(wbh_b), full(wbd_b), full(wbm_b), full(wo_b),
                  full(g1), full(b1), full(wr_cat), full(tri)],
        out_specs=(row(d), row(LANES), pl.BlockSpec((SUBLANES, tm), lambda i: (0, i)),
                   pl.BlockSpec((1, LANES), lambda i: (0, 0))),
        scratch_shapes=[pltpu.VMEM((1, LANES), F32)],
        compiler_params=pltpu.CompilerParams(
            dimension_semantics=("arbitrary",), vmem_limit_bytes=V7X_VMEM_LIMIT),
        name="merge",
    )(x2, yh, yd, ym, wg_b, wbh_b, wbd_b, wbm_b, wo_b, g1, b1, wr_cat, tri)


def _scatter_kernel(pend_ref, psize_ref, nv_ref, dest_ref, h_ref, xs_hbm, dest_smem, zero_ref,
                    row_sem, aux_sem, *, ts, rows):
    cp = pltpu.make_async_copy(dest_ref, dest_smem, aux_sem)
    cp.start()
    cp.wait()
    n_blocks = xs_hbm.shape[0] // rows

    @pl.when(pl.program_id(0) == 0)
    def _():
        zero_ref[...] = jnp.zeros_like(zero_ref)

        def fill(start):
            return pltpu.make_async_copy(zero_ref, xs_hbm.at[pl.ds(start, rows)], aux_sem)

        def fills(act):
            for e in range(N_EXPERTS):
                @pl.when(psize_ref[e] > 0)
                def _():
                    act(fill(pl.multiple_of(pend_ref[e] - rows, rows)))
            for blk in range(n_blocks - N_EXPERTS, n_blocks):
                @pl.when(blk >= nv_ref[0])
                def _():
                    act(fill(blk * rows))

        fills(lambda c: c.start())
        fills(lambda c: c.wait())

    def issue(t, carry):
        for k in range(TOP_K):
            d = dest_smem[0, 0, k * ts + t]
            pltpu.make_async_copy(h_ref.at[pl.ds(t, 1)], xs_hbm.at[pl.ds(d, 1)], row_sem).start()
        return carry

    lax.fori_loop(0, ts, issue, 0, unroll=8)
    pltpu.make_async_copy(xs_hbm.at[pl.ds(0, TOP_K * ts)], xs_hbm.at[pl.ds(0, TOP_K * ts)],
                          row_sem).wait()


def _scatter(padded_end, padded, n_valid, dest3, h, *, ts, rows, n_slots):
    n_tok, d = h.shape
    return pl.pallas_call(
        functools.partial(_scatter_kernel, ts=ts, rows=rows),
        out_shape=jax.ShapeDtypeStruct((n_slots, d), h.dtype),
        grid_spec=pltpu.PrefetchScalarGridSpec(
            num_scalar_prefetch=3,
            grid=(n_tok // ts,),
            in_specs=[pl.BlockSpec((1, 1, TOP_K * ts), lambda i, pe, ps, nv: (i, 0, 0)),
                      pl.BlockSpec((ts, d), lambda i, pe, ps, nv: (i, 0))],
            out_specs=pl.BlockSpec(memory_space=pl.ANY),
            scratch_shapes=[pltpu.SMEM((1, 1, TOP_K * ts), jnp.int32),
                            pltpu.VMEM((rows, d), h.dtype),
                            pltpu.SemaphoreType.DMA, pltpu.SemaphoreType.DMA],
        ),
        compiler_params=pltpu.CompilerParams(
            dimension_semantics=("arbitrary",), vmem_limit_bytes=V7X_VMEM_LIMIT),
        name="scatter",
    )(padded_end, padded, n_valid, dest3, h)


def _experts_kernel(be_ref, nv_ref, x_ref, wg_ref, wu_ref, wd_ref, y_ref, wg_b, wu_b, wd_b):
    j = pl.program_id(0)
    new_expert = (j == 0) | (be_ref[j] != be_ref[jnp.maximum(j - 1, 0)])

    @pl.when(new_expert & (j < nv_ref[0]))
    def _():
        wg_b[...] = wg_ref[...].astype(BF16)
        wu_b[...] = wu_ref[...].astype(BF16)
        wd_b[...] = wd_ref[...].astype(BF16)

    @pl.when(j < nv_ref[0])
    def _():
        half = x_ref.shape[0] // 2
        acts = []
        for p in range(2):
            x = x_ref[p * half:(p + 1) * half, :].astype(BF16)
            g = _dot(x, wg_b[...])
            u = _dot(x, wu_b[...])
            acts.append((g * _sigmoid(g) * u).astype(BF16))
        for p in range(2):
            y_ref[p * half:(p + 1) * half, :] = _dot(acts[p], wd_b[...])

    @pl.when(j >= nv_ref[0])
    def _():
        y_ref[...] = jnp.zeros_like(y_ref)


def _experts(block_expert, n_valid, xs, w_gate, w_up, w_down, *, rows):
    n_slots, d = xs.shape
    de = w_gate.shape[2]
    return pl.pallas_call(
        _experts_kernel,
        out_shape=jax.ShapeDtypeStruct((n_slots, d), F32),
        grid_spec=pltpu.PrefetchScalarGridSpec(
            num_scalar_prefetch=2,
            grid=(n_slots // rows,),
            in_specs=[pl.BlockSpec((rows, d), lambda j, be, nv: (jnp.minimum(j, nv[0] - 1), 0)),
                      pl.BlockSpec((None, d, de), lambda j, be, nv: (be[j], 0, 0)),
                      pl.BlockSpec((None, d, de), lambda j, be, nv: (be[j], 0, 0)),
                      pl.BlockSpec((None, de, d), lambda j, be, nv: (be[j], 0, 0))],
            out_specs=pl.BlockSpec((rows, d), lambda j, be, nv: (j, 0)),
            scratch_shapes=[pltpu.VMEM((d, de), BF16), pltpu.VMEM((d, de), BF16),
                            pltpu.VMEM((de, d), BF16)],
        ),
        compiler_params=pltpu.CompilerParams(
            dimension_semantics=("arbitrary",), vmem_limit_bytes=V7X_VMEM_LIMIT),
        name="experts",
    )(block_expert, n_valid, xs, w_gate, w_up, w_down)


def _combine_kernel(dest_ref, dest_next_ref, h_ref, meta_ref, g2_ref, b2_ref, y_hbm, o_ref,
                    dest_smem0, dest_smem1, ybuf0, ybuf1, row_sem, idx_sem, *, tc):
    i = pl.program_id(0)
    slot = i % 2
    dest_smem = (dest_smem0, dest_smem1)
    ybuf = (ybuf0, ybuf1)

    def start_gathers(idx_ref, s):
        cp = pltpu.make_async_copy(idx_ref, dest_smem[s], idx_sem)
        cp.start()
        cp.wait()

        def issue(t, carry):
            for k in range(TOP_K):
                d = dest_smem[s][0, 0, k * tc + t]
                pltpu.make_async_copy(y_hbm.at[pl.ds(d, 1)], ybuf[s].at[k, pl.ds(t, 1)],
                                      row_sem.at[s]).start()
            return carry

        lax.fori_loop(0, tc, issue, 0, unroll=8)

    @pl.when(i == 0)
    def _():
        start_gathers(dest_ref, 0)

    for s in range(2):
        @pl.when((i + 1 < pl.num_programs(0)) & (slot == 1 - s))
        def _():
            start_gathers(dest_next_ref, s)

    for s in range(2):
        @pl.when(slot == s)
        def _():
            for k in range(TOP_K):
                pltpu.make_async_copy(y_hbm.at[pl.ds(0, tc)], ybuf[s].at[k], row_sem.at[s]).wait()
            meta = meta_ref[...]
            ffn = meta[:, 4:5] * ybuf[s][0] + meta[:, 5:6] * ybuf[s][1]
            o_ref[...] = _layer_norm(DEEPNORM_ALPHA * h_ref[...] + ffn, g2_ref[...], b2_ref[...])


def _combine(dest3, h, meta, g2, b2, y, *, tc):
    n_tok, d = h.shape
    n_steps = n_tok // tc
    return pl.pallas_call(
        functools.partial(_combine_kernel, tc=tc),
        out_shape=jax.ShapeDtypeStruct((n_tok, d), F32),
        grid=(n_steps,),
        in_specs=[pl.BlockSpec((1, 1, TOP_K * tc), lambda i: (i, 0, 0)),
                  pl.BlockSpec((1, 1, TOP_K * tc),
                               lambda i: (jnp.minimum(i + 1, n_steps - 1), 0, 0)),
                  pl.BlockSpec((tc, d), lambda i: (i, 0)),
                  pl.BlockSpec((tc, LANES), lambda i: (i, 0)),
                  pl.BlockSpec(g2.shape, lambda i: (0, 0)),
                  pl.BlockSpec(b2.shape, lambda i: (0, 0)),
                  pl.BlockSpec(memory_space=pl.ANY)],
        out_specs=pl.BlockSpec((tc, d), lambda i: (i, 0)),
        scratch_shapes=[pltpu.SMEM((1, 1, TOP_K * tc), jnp.int32),
                        pltpu.SMEM((1, 1, TOP_K * tc), jnp.int32),
                        pltpu.VMEM((TOP_K, tc, d), F32),
                        pltpu.VMEM((TOP_K, tc, d), F32),
                        pltpu.SemaphoreType.DMA((2,)),
                        pltpu.SemaphoreType.DMA],
        compiler_params=pltpu.CompilerParams(
            dimension_semantics=("arbitrary",), vmem_limit_bytes=V7X_VMEM_LIMIT),
        name="combine",
    )(dest3, dest3, h, meta, g2, b2, y)


def _w_in_column_order():
    hgw = HG_HEADS * HG_DK
    qkw = DF_HEADS * DF_DQK
    base = 4 * hgw
    cols = list(range(base))
    for first in (base, base + 2 * qkw):
        for h in range(DF_HEADS):
            cols += list(range(first + h * DF_DQK, first + (h + 1) * DF_DQK))
            cols += list(range(first + qkw + h * DF_DQK, first + qkw + (h + 1) * DF_DQK))
    total = base + 4 * qkw + DF_HEADS * DF_DV + MEM_HEADS * MEM_DH
    cols += list(range(base + 4 * qkw, total))
    return jnp.asarray(cols, jnp.int32)


def kernel(x, mem, positions, w_in, w_gates, hgrn_lower_bounds, hgrn_norm_gain, diff_lambda_q1, diff_lambda_k1, diff_lambda_q2, diff_lambda_k2, diff_subln_gain, w_mem_kv, w_branch_hgrn, w_branch_diff, w_branch_mem, w_out, ln1_gain, ln1_bias, w_group_router, w_expert_router, w_expert_gate, w_expert_up, w_expert_down, ln2_gain, ln2_bias):
    b, s, d = x.shape
    n_tok = b * s
    t = _tiles(s, n_tok)
    assert w_in.shape[0] == DEPTH
    layer = 0
    lam_init = 0.8 - 0.6 * math.exp(-0.3 * layer)
    hgw = HG_HEADS * HG_DK

    w_in_b = jnp.take(w_in[layer], _w_in_column_order(), axis=1).astype(BF16)
    inv_freq = ROPE_THETA ** (-jnp.arange(0, ROT_DIM, 2, dtype=F32) / ROT_DIM)
    invf8 = inv_freq.reshape(ROT_DIM // 2, 1)
    w_router = jnp.concatenate(
        [w_group_router[layer], w_expert_router[layer],
         jnp.zeros((d, LANES - N_GROUPS - N_EXPERTS), F32)], axis=1)
    wr_hi = w_router.astype(BF16)
    wr_lo = (w_router - wr_hi.astype(F32)).astype(BF16)
    wr_cat = jnp.concatenate([wr_hi, wr_lo], axis=1)

    x2 = x.reshape(n_tok, d)
    pos3 = positions.reshape(n_tok // t["proj_rows"], 1, t["proj_rows"])
    p, logf = _proj(x2, pos3, w_in_b, hgrn_lower_bounds, invf8, tm=t["proj_rows"], layer=layer)
    p3 = p.reshape(b, s, p.shape[1])

    y_hg = _hgrn(p3, logf.reshape(b, s, hgw), hgrn_norm_gain[layer].reshape(1, hgw),
                 rows=t["hgrn_rows"], par=t["hgrn_par"])
    y_df = _diffattn(p3, diff_lambda_q1[layer].reshape(1, -1), diff_lambda_k1[layer].reshape(1, -1),
                     diff_lambda_q2[layer].reshape(1, -1), diff_lambda_k2[layer].reshape(1, -1),
                     diff_subln_gain[layer].reshape(1, -1), tq=t["attn_q"], tk=t["attn_k"], rq=t["attn_rows"],
                     lam_init=lam_init, col0=4 * hgw // LANES)
    y_mem = _memattn(p3, mem, w_mem_kv[layer].astype(BF16), tq=t["mem_q"],
                     col0=7 * hgw // (MEM_HEADS * MEM_DH))

    h, meta, meta_t, counts_f = _merge(
        x2, y_hg.reshape(n_tok, -1), y_df.reshape(n_tok, -1), y_mem.reshape(n_tok, -1),
        w_gates[layer].astype(BF16), w_branch_hgrn[layer].astype(BF16),
        w_branch_diff[layer].astype(BF16), w_branch_mem[layer].astype(BF16),
        w_out[layer].astype(BF16), ln1_gain[layer].reshape(1, d), ln1_bias[layer].reshape(1, d),
        wr_cat, tm=t["merge_rows"], parts=t["merge_parts"])

    rows = t["moe_rows"]
    counts = counts_f[0, N_GROUPS:N_GROUPS + N_EXPERTS].astype(jnp.int32)
    padded = (counts + rows - 1) // rows * rows
    padded_end = jnp.cumsum(padded)
    padded_start = padded_end - padded
    expert_id = meta_t[0:TOP_K].astype(jnp.int32)
    rank = meta_t[TOP_K:2 * TOP_K].astype(jnp.int32)
    dest = rank
    for e in range(N_EXPERTS):
        dest = dest + jnp.where(expert_id == e, padded_start[e], 0)

    def dest_tiles(tile):
        return dest.reshape(TOP_K, n_tok // tile, tile).transpose(1, 0, 2).reshape(
            n_tok // tile, 1, TOP_K * tile)
    n_slots = n_tok * TOP_K + N_EXPERTS * rows
    n_blocks = n_slots // rows
    block_start = jnp.arange(n_blocks, dtype=jnp.int32) * rows
    block_expert = jnp.minimum(
        jnp.sum((padded_end[None, :] <= block_start[:, None]).astype(jnp.int32), axis=1),
        N_EXPERTS - 1)
    n_valid = (padded_end[-1:] // rows).astype(jnp.int32)

    ts = t["scatter_rows"]
    xs = _scatter(padded_end.astype(jnp.int32), padded, n_valid,
                  dest_tiles(ts), h, ts=ts, rows=rows, n_slots=n_slots)
    y = _experts(block_expert, n_valid, xs, w_expert_gate[layer],
                 w_expert_up[layer], w_expert_down[layer], rows=rows)
    tc = t["combine_rows"]
    out = _combine(dest_tiles(tc), h, meta,
                   ln2_gain[layer].reshape(1, d), ln2_bias[layer].reshape(1, d), y, tc=tc)
    return out.reshape(b, s, d)
```

```python
import functools
import math

import jax
import jax.numpy as jnp
from jax import lax
from jax.experimental import pallas as pl
from jax.experimental.pallas import tpu as pltpu

F32 = jnp.float32
BF16 = jnp.bfloat16

HG_HEADS = 4
HG_DK = 128
HG_CHUNK = 64
DF_HEADS = 4
DF_DQK = 64
DF_DV = 128
ROPE_THETA = 500000.0
ROT_DIM = 16
MEM_HEADS = 4
MEM_DH = 128
N_BRANCH = 3
N_GROUPS = 4
EXPERTS_PER_GROUP = 8
N_EXPERTS = 32
TOP_K = 2
DEPTH = 1
DEEPNORM_ALPHA = (2.0 * DEPTH) ** 0.25
LN_EPS = 1e-5
RMS_EPS = 1e-6
LANES = 128
SUBLANES = 8
V7X_VMEM_LIMIT = 56 * 1024 * 1024


def _tiles(seq, n_tok):
    return dict(
        proj_rows=min(1024, n_tok),
        hgrn_rows=min(512, seq),
        hgrn_par=4,
        attn_q=min(2048, seq),
        attn_k=min(1024, seq),
        attn_rows=256,
        mem_q=min(512, seq),
        merge_rows=min(1024, n_tok),
        merge_parts=4,
        scatter_rows=min(1024, n_tok),
        moe_rows=512,
        combine_rows=min(512, n_tok),
    )


def _sigmoid(v):
    return 1.0 / (1.0 + jnp.exp(-v))


def _dot(a, b):
    return jnp.dot(a, b, preferred_element_type=F32)


def _dot_nt(a, b):
    return lax.dot_general(a, b, (((1,), (1,)), ((), ())), preferred_element_type=F32)


def _dot_tn(a, b):
    return lax.dot_general(a, b, (((0,), (0,)), ((), ())), preferred_element_type=F32)


def _proj_kernel(x_ref, pos_ref, w_ref, lbraw_ref, invf_ref, p_ref, logf_ref, *, layer):
    tm = x_ref.shape[0]
    wd = logf_ref.shape[1]
    x = x_ref[...].astype(BF16)

    def mm(j):
        return _dot(x, w_ref[:, j * wd:(j + 1) * wd])

    a = lbraw_ref[...]
    e = jnp.exp(a - jnp.max(a, axis=0, keepdims=True))
    sm = e / jnp.sum(e, axis=0, keepdims=True)
    lb = jnp.sum(sm[0:layer + 1, :], axis=0, keepdims=True)

    hq = mm(0)
    p_ref[:, 0:wd] = (hq * _sigmoid(hq)).astype(BF16)
    hf = mm(1)
    forget = lb + (1.0 - lb) * _sigmoid(hf)
    logf_ref[...] = jnp.log(forget)
    p_ref[:, wd:2 * wd] = (1.0 - forget).astype(BF16)
    p_ref[:, 2 * wd:3 * wd] = mm(2).astype(BF16)
    hg = mm(3)
    p_ref[:, 3 * wd:4 * wd] = (hg * _sigmoid(hg)).astype(BF16)

    ang = invf_ref[...] * pos_ref[...].astype(F32)
    c8 = jnp.cos(ang)
    s8 = jnp.sin(ang)
    one = jnp.ones((DF_DQK - ROT_DIM, tm), F32)
    zero = jnp.zeros((DF_DQK - ROT_DIM, tm), F32)
    z8 = jnp.zeros_like(s8)
    cos_t = jnp.concatenate([c8, c8, one, c8, c8, one], axis=0).T
    sin_lo = jnp.concatenate([-s8, z8, zero, -s8, z8, zero], axis=0).T
    sin_hi = jnp.concatenate([z8, s8, zero, z8, s8, zero], axis=0).T
    half = ROT_DIM // 2

    def rope(t):
        return (t * cos_t + pltpu.roll(t, LANES - half, 1) * sin_lo
                + pltpu.roll(t, half, 1) * sin_hi)

    q = mm(4)
    k = mm(5)
    for j in range(wd // LANES):
        sl = slice(j * LANES, (j + 1) * LANES)
        p_ref[:, 4 * wd + j * LANES:4 * wd + (j + 1) * LANES] = (
            rope(q[:, sl]) * (DF_DQK ** -0.5 * math.log2(math.e))).astype(BF16)
        p_ref[:, 5 * wd + j * LANES:5 * wd + (j + 1) * LANES] = rope(k[:, sl]).astype(BF16)
    p_ref[:, 6 * wd:7 * wd] = mm(6).astype(BF16)
    p_ref[:, 7 * wd:8 * wd] = mm(7).astype(BF16)


def _proj(x2, pos3, w_in_b, lbraw, invf8, *, tm, layer):
    n_tok, d = x2.shape
    width = w_in_b.shape[1]
    wd = lbraw.shape[1]
    return pl.pallas_call(
        functools.partial(_proj_kernel, layer=layer),
        out_shape=(jax.ShapeDtypeStruct((n_tok, width), BF16),
                   jax.ShapeDtypeStruct((n_tok, wd), F32)),
        grid=(n_tok // tm,),
        in_specs=[
            pl.BlockSpec((tm, d), lambda i: (i, 0)),
            pl.BlockSpec((None, 1, tm), lambda i: (i, 0, 0)),
            pl.BlockSpec((d, width), lambda i: (0, 0)),
            pl.BlockSpec(lbraw.shape, lambda i: (0, 0)),
            pl.BlockSpec(invf8.shape, lambda i: (0, 0)),
        ],
        out_specs=(pl.BlockSpec((tm, width), lambda i: (i, 0)),
                   pl.BlockSpec((tm, wd), lambda i: (i, 0))),
        compiler_params=pltpu.CompilerParams(
            dimension_semantics=("arbitrary",), vmem_limit_bytes=V7X_VMEM_LIMIT),
        name="proj",
    )(x2, pos3, w_in_b, lbraw, invf8)


def _hgrn_kernel(q_ref, k_ref, v_ref, g_ref, lf_ref, gain_ref, lvl_ref, tri_ref, o_ref, st_ref, *,
                 par):
    rows = q_ref.shape[0]
    c = HG_CHUNK
    dk = HG_DK

    @pl.when(pl.program_id(1) == 0)
    def _():
        st_ref[...] = jnp.zeros_like(st_ref)

    lvl = lvl_ref[...]
    tri = tri_ref[...]
    sub = lax.broadcasted_iota(jnp.int32, (c // 8, 8, dk), 1)
    row = lax.broadcasted_iota(jnp.int32, (c, dk), 0)

    def anchors(cum):
        out = []
        for m in (32, 16, 8):
            pieces = []
            for j in range(c // (2 * m)):
                a = j * 2 * m + m - 1
                pieces.append(jnp.broadcast_to(cum[a:a + 1, :], (2 * m, dk)))
            out.append(pieces[0] if len(pieces) == 1 else jnp.concatenate(pieces, axis=0))
        c8 = cum.reshape(c // 8, 8, dk)
        out.append(jnp.broadcast_to(c8[:, 3:4, :], c8.shape).reshape(c, dk))
        a2 = jnp.where(sub < 4, jnp.broadcast_to(c8[:, 1:2, :], c8.shape),
                       jnp.broadcast_to(c8[:, 5:6, :], c8.shape))
        out.append(a2.reshape(c, dk))
        return out

    def chunk_body(ci, carry):
        units = [(cc, h) for cc in range(par) for h in range(HG_HEADS)]
        r0 = [pl.multiple_of((ci * par + cc) * c, c) for cc in range(par)]
        hsl = [slice(h * dk, (h + 1) * dk) for h in range(HG_HEADS)]
        q = {u: q_ref[pl.ds(r0[u[0]], c), hsl[u[1]]].astype(F32) for u in units}
        k = {u: k_ref[pl.ds(r0[u[0]], c), hsl[u[1]]].astype(F32) for u in units}
        v = {u: v_ref[pl.ds(r0[u[0]], c), hsl[u[1]]] for u in units}
        lf = {u: lf_ref[pl.ds(r0[u[0]], c), hsl[u[1]]] for u in units}
        cum = {}
        for u in units:
            lf1 = lf[u].astype(BF16)
            r1 = lf[u] - lf1.astype(F32)
            lf2 = r1.astype(BF16)
            lf3 = (r1 - lf2.astype(F32)).astype(BF16)
            cum[u] = _dot(tri, lf1) + _dot(tri, lf2) + _dot(tri, lf3)
        operands = {}
        for u in units:
            e_lvls = [jnp.exp(-jnp.abs(cum[u] - a)) for a in anchors(cum[u])]
            e_lvls.append(jnp.where((row & 1) == 1, jnp.exp(lf[u]), 1.0))
            ops = [((q[u] * e).astype(BF16), (k[u] * e).astype(BF16)) for e in e_lvls]
            ops.append((q[u].astype(BF16), k[u].astype(BF16)))
            operands[u] = ops
        ecum = {u: jnp.exp(cum[u]) for u in units}
        k_dec = {u: (k[u] * jnp.exp(cum[u][c - 1:c, :] - cum[u])).astype(BF16) for u in units}
        q_dec = {u: (q[u] * ecum[u]).astype(BF16) for u in units}
        level_dots = {u: [_dot_nt(qe, ke) for qe, ke in operands[u]] for u in units}
        st = [st_ref[h] for h in range(HG_HEADS)]
        o_inter = {}
        for u in units:
            h = u[1]
            o_inter[u] = _dot_nt(q_dec[u], st[h].astype(BF16))
            st[h] = st[h] * ecum[u][c - 1:c, :] + _dot_tn(v[u], k_dec[u])
        for h in range(HG_HEADS):
            st_ref[h] = st[h]
        scores = {}
        for u in units:
            sc = jnp.zeros((c, c), F32)
            for i, d in enumerate(level_dots[u]):
                sc = jnp.where(lvl == i, d, sc)
            scores[u] = sc.astype(BF16)
        for u in units:
            o = o_inter[u] + _dot(scores[u], v[u])
            y = o * lax.rsqrt(jnp.mean(o * o, axis=-1, keepdims=True) + RMS_EPS)
            y = y * gain_ref[:, hsl[u[1]]] * g_ref[pl.ds(r0[u[0]], c), hsl[u[1]]].astype(F32)
            o_ref[pl.ds(r0[u[0]], c), hsl[u[1]]] = y.astype(o_ref.dtype)
        return carry

    lax.fori_loop(0, rows // (c * par), chunk_body, 0)


def _hgrn_consts():
    c = HG_CHUNK
    t = jnp.arange(c)[:, None]
    s = jnp.arange(c)[None, :]
    x = t ^ s
    lvl = jnp.full((c, c), -1, jnp.int32)
    for i, m in enumerate((32, 16, 8, 4, 2, 1)):
        lvl = jnp.where((t > s) & (x >= m) & (x < 2 * m), i, lvl)
    lvl = jnp.where(t == s, 6, lvl)
    tri = (t >= s).astype(BF16)
    return lvl, tri


def _hgrn(p3, logf3, gain, *, rows, par):
    b, s, _ = p3.shape
    wd = logf3.shape[2]
    lvl, tri = _hgrn_consts()

    def col(j):
        return pl.BlockSpec((None, rows, wd), lambda bi, si: (bi, si, j))

    return pl.pallas_call(
        functools.partial(_hgrn_kernel, par=par),
        out_shape=jax.ShapeDtypeStruct((b, s, wd), BF16),
        grid=(b, s // rows),
        in_specs=[col(0), col(1), col(2), col(3), col(0),
                  pl.BlockSpec(gain.shape, lambda bi, si: (0, 0)),
                  pl.BlockSpec(lvl.shape, lambda bi, si: (0, 0)),
                  pl.BlockSpec(tri.shape, lambda bi, si: (0, 0))],
        out_specs=col(0),
        scratch_shapes=[pltpu.VMEM((HG_HEADS, HG_DK, HG_DK), F32)],
        compiler_params=pltpu.CompilerParams(
            dimension_semantics=("arbitrary", "arbitrary"), vmem_limit_bytes=V7X_VMEM_LIMIT),
        name="hgrn",
    )(p3, p3, p3, p3, logf3, gain, lvl, tri)


def _diffattn_kernel(q_ref, k_ref, v_ref, lq1_ref, lk1_ref, lq2_ref, lk2_ref, gain_ref, o_ref,
                     m_ref, l_ref, acc_ref, *, tk, rq, lam_init):
    tq = q_ref.shape[0]
    qi = pl.program_id(2)
    lane = lax.broadcasted_iota(jnp.int32, (tq, LANES), 1)
    q = q_ref[...]
    zero = jnp.zeros_like(q)
    qm = (jnp.where(lane < DF_DQK, q, zero), jnp.where(lane >= DF_DQK, q, zero))

    m_ref[...] = jnp.full(m_ref.shape, -jnp.inf, F32)
    l_ref[...] = jnp.zeros(l_ref.shape, F32)
    acc_ref[...] = jnp.zeros(acc_ref.shape, F32)

    def tile(k0, diag):
        rb = LANES if diag else rq
        blocks = [(r, j) for r in range(tq // rb) for j in range(2)]
        n_keys = {r: ((r + 1) * rb if diag else tk) for r in range(tq // rb)}
        scores = {}
        for r, j in blocks:
            rows = slice(r * rb, (r + 1) * rb)
            scores[r, j] = _dot_nt(qm[j][rows], k_ref[pl.ds(k0, n_keys[r]), :])
        probs = {}
        for r, j in blocks:
            rows = slice(r * rb, (r + 1) * rb)
            s = scores[r, j]
            chunks = [s[:, c * LANES:(c + 1) * LANES] for c in range(n_keys[r] // LANES)]
            if diag:
                n_d = rb // LANES
                rr = lax.broadcasted_iota(jnp.int32, (rb, LANES), 0)
                cc = lax.broadcasted_iota(jnp.int32, (rb, LANES), 1)
                for c in range(n_d):
                    idx = len(chunks) - n_d + c
                    chunks[idx] = jnp.where(rr >= cc + c * LANES, chunks[idx], -jnp.inf)
            cmax = functools.reduce(jnp.maximum, chunks)
            m_old = m_ref[j, rows]
            m_new = jnp.maximum(m_old, jnp.max(cmax, axis=-1, keepdims=True))
            alpha = jnp.exp2(m_old - m_new)
            ps = [jnp.exp2(ch - m_new) for ch in chunks]
            l_ref[j, rows] = alpha * l_ref[j, rows] + functools.reduce(lambda a, b: a + b, ps)
            m_ref[j, rows] = m_new
            acc_ref[j, rows] = alpha * acc_ref[j, rows]
            probs[r, j] = jnp.concatenate([x.astype(BF16) for x in ps], axis=1)
        for r, j in blocks:
            rows = slice(r * rb, (r + 1) * rb)
            acc_ref[j, rows] += _dot(probs[r, j], v_ref[pl.ds(k0, n_keys[r]), :])

    def body(ki, carry):
        tile(pl.multiple_of(ki * tk, tk), False)
        return carry

    lax.fori_loop(0, (qi * tq) // tk, body, 0)
    tile(pl.multiple_of(qi * tq, tq), True)

    lam = (jnp.exp(jnp.sum(lq1_ref[...] * lk1_ref[...], keepdims=True))
           - jnp.exp(jnp.sum(lq2_ref[...] * lk2_ref[...], keepdims=True)) + lam_init)
    l0 = jnp.sum(l_ref[0], axis=-1, keepdims=True)
    l1 = jnp.sum(l_ref[1], axis=-1, keepdims=True)
    o = acc_ref[0] / l0 - lam * (acc_ref[1] / l1)
    y = o * lax.rsqrt(jnp.mean(o * o, axis=-1, keepdims=True) + RMS_EPS)
    o_ref[...] = (y * gain_ref[...] * (1.0 - lam_init)).astype(o_ref.dtype)


def _diffattn(p3, lq1, lk1, lq2, lk2, gain, *, tq, tk, rq, lam_init, col0):
    b, s, _ = p3.shape
    assert tq % tk == 0 and tq % rq == 0 and rq % LANES == 0
    small = pl.BlockSpec((1, DF_DQK), lambda bi, hi, qi: (0, 0))
    return pl.pallas_call(
        functools.partial(_diffattn_kernel, tk=tk, rq=rq, lam_init=lam_init),
        out_shape=jax.ShapeDtypeStruct((b, s, DF_HEADS * DF_DV), BF16),
        grid=(b, DF_HEADS, s // tq),
        in_specs=[
            pl.BlockSpec((None, tq, LANES), lambda bi, hi, qi: (bi, qi, col0 + hi)),
            pl.BlockSpec((None, s, LANES), lambda bi, hi, qi: (bi, 0, col0 + DF_HEADS + hi)),
            pl.BlockSpec((None, s, LANES), lambda bi, hi, qi: (bi, 0, col0 + 2 * DF_HEADS + hi)),
            small, small, small, small,
            pl.BlockSpec((1, DF_DV), lambda bi, hi, qi: (0, 0)),
        ],
        out_specs=pl.BlockSpec((None, tq, DF_DV), lambda bi, hi, qi: (bi, qi, hi)),
        scratch_shapes=[pltpu.VMEM((2, tq, LANES), F32), pltpu.VMEM((2, tq, LANES), F32),
                        pltpu.VMEM((2, tq, DF_DV), F32)],
        compiler_params=pltpu.CompilerParams(
            dimension_semantics=("arbitrary", "arbitrary", "arbitrary"),
            vmem_limit_bytes=V7X_VMEM_LIMIT),
        name="diffattn",
    )(p3, p3, p3, lq1, lk1, lq2, lk2, gain)


def _memattn_kernel(q_ref, mem_ref, w_ref, o_ref, kv_ref):
    width = MEM_HEADS * MEM_DH

    @pl.when(pl.program_id(1) == 0)
    def _():
        kv_ref[...] = _dot(mem_ref[...].astype(BF16), w_ref[...]).astype(BF16)

    for h in range(MEM_HEADS):
        hs = slice(h * MEM_DH, (h + 1) * MEM_DH)
        s = _dot_nt(q_ref[:, hs], kv_ref[:, hs]) * (MEM_DH ** -0.5)
        e = jnp.exp(s - jnp.max(s, axis=-1, keepdims=True))
        p = e / jnp.sum(e, axis=-1, keepdims=True)
        o_ref[:, hs] = _dot(p.astype(BF16), kv_ref[:, width + h * MEM_DH:width + (h + 1) * MEM_DH]
                            ).astype(o_ref.dtype)


def _memattn(p3, mem, w_kv_b, *, tq, col0):
    b, s, _ = p3.shape
    n_mem, d = mem.shape[1], mem.shape[2]
    width = MEM_HEADS * MEM_DH
    return pl.pallas_call(
        _memattn_kernel,
        out_shape=jax.ShapeDtypeStruct((b, s, width), BF16),
        grid=(b, s // tq),
        in_specs=[
            pl.BlockSpec((None, tq, width), lambda bi, qi: (bi, qi, col0)),
            pl.BlockSpec((None, n_mem, d), lambda bi, qi: (bi, 0, 0)),
            pl.BlockSpec(w_kv_b.shape, lambda bi, qi: (0, 0)),
        ],
        out_specs=pl.BlockSpec((None, tq, width), lambda bi, qi: (bi, qi, 0)),
        scratch_shapes=[pltpu.VMEM((n_mem, 2 * width), BF16)],
        compiler_params=pltpu.CompilerParams(
            dimension_semantics=("arbitrary", "arbitrary"), vmem_limit_bytes=V7X_VMEM_LIMIT),
        name="memattn",
    )(p3, mem, w_kv_b)


def _layer_norm(z, gain, bias):
    mu = jnp.mean(z, axis=-1, keepdims=True)
    zc = z - mu
    var = jnp.mean(zc * zc, axis=-1, keepdims=True)
    return zc * lax.rsqrt(var + LN_EPS) * gain + bias


def _merge_kernel(x_ref, yh_ref, yd_ref, ym_ref, wg_ref, wbh_ref, wbd_ref, wbm_ref, wo_ref,
                  g1_ref, b1_ref, wr_cat_ref, tri_ref, h_ref, meta_ref, meta_t_ref, cnt_ref, run_ref,
                  *, parts):
    tm, d = x_ref.shape
    rp = tm // parts

    @pl.when(pl.program_id(0) == 0)
    def _():
        run_ref[...] = jnp.zeros_like(run_ref)

    lane = lax.broadcasted_iota(jnp.int32, (rp, LANES), 1)
    neg = jnp.full((rp, LANES), -jnp.inf, F32)
    big = jnp.full((rp, LANES), 4 * LANES, jnp.int32)
    is_g = lane < N_GROUPS
    branches = ((yh_ref, wbh_ref), (yd_ref, wbd_ref), (ym_ref, wbm_ref))

    def project(p):
        rows = slice(p * rp, (p + 1) * rp)
        x = x_ref[rows, :]
        xb = x.astype(BF16)
        merged = None
        for j, (y_ref, wb_ref) in enumerate(branches):
            gate = _sigmoid(_dot(xb, wg_ref[:, j * d:(j + 1) * d]))
            term = gate * _dot(y_ref[rows, :], wb_ref[...])
            merged = term if merged is None else merged + term
        return x, merged.astype(BF16)

    def norm(p, x, mix):
        h = _layer_norm(DEEPNORM_ALPHA * x + mix, g1_ref[...], b1_ref[...])
        h_ref[p * rp:(p + 1) * rp, :] = h
        h_hi = h.astype(BF16)
        h_lo = (h - h_hi.astype(F32)).astype(BF16)
        return h_hi, h_lo

    def logits(h_hi, h_lo):
        t = _dot(h_hi, wr_cat_ref[...])
        return t[:, :LANES] + t[:, LANES:] + _dot(h_lo, wr_cat_ref[:, :LANES])

    def route(lg):
        g_max = jnp.max(jnp.where(is_g, lg, neg), axis=-1, keepdims=True)
        g_sum = jnp.sum(jnp.where(is_g, jnp.exp(lg - g_max), 0.0), axis=-1, keepdims=True)
        group_w = 1.0 / g_sum
        g_idx = jnp.min(jnp.where(is_g & (lg == g_max), lane, big), axis=-1, keepdims=True)
        in_grp = ((lane >= N_GROUPS) & (lane < N_GROUPS + N_EXPERTS)
                  & (jnp.right_shift(lane - N_GROUPS, 3) == g_idx))
        v1 = jnp.max(jnp.where(in_grp, lg, neg), axis=-1, keepdims=True)
        i1 = jnp.min(jnp.where(in_grp & (lg == v1), lane, big), axis=-1, keepdims=True)
        rest = in_grp & (lane != i1)
        v2 = jnp.max(jnp.where(rest, lg, neg), axis=-1, keepdims=True)
        i2 = jnp.min(jnp.where(rest & (lg == v2), lane, big), axis=-1, keepdims=True)
        e21 = jnp.exp(v2 - v1)
        w1 = group_w / (1.0 + e21)
        w2 = group_w * e21 / (1.0 + e21)
        return i1, i2, w1, w2

    def finish(p, routed, run):
        i1, i2, w1, w2 = routed
        hot1 = lane == i1
        hot2 = lane == i2
        hot = jnp.where(hot1 | hot2, 1.0, 0.0)
        before = _dot(tri_ref[...], hot.astype(BF16)) + run
        r1 = jnp.sum(jnp.where(hot1, before, 0.0), axis=-1, keepdims=True)
        r2 = jnp.sum(jnp.where(hot2, before, 0.0), axis=-1, keepdims=True)
        meta = jnp.zeros((rp, LANES), F32)
        for col, val in enumerate(((i1 - N_GROUPS).astype(F32), (i2 - N_GROUPS).astype(F32),
                                   r1, r2, w1, w2)):
            meta = jnp.where(lane == col, val, meta)
        meta_ref[p * rp:(p + 1) * rp, :] = meta
        meta_t_ref[:, p * rp:(p + 1) * rp] = meta.T[:SUBLANES, :]
        return before[rp - 1:rp, :] + hot[rp - 1:rp, :]

    run = run_ref[...]
    state = {}
    for step in range(parts + 4):
        p = step - 4
        if 0 <= p < parts:
            run = finish(p, state.pop(("routed", p)), run)
        p = step - 3
        if 0 <= p < parts:
            state["routed", p] = route(state.pop(("lg", p)))
        p = step - 2
        if 0 <= p < parts:
            state["lg", p] = logits(*state.pop(("h", p)))
        p = step - 1
        if 0 <= p < parts:
            x, merged = state.pop(("proj", p))
            state["h", p] = norm(p, x, _dot(merged, wo_ref[...]))
        p = step
        if 0 <= p < parts:
            state["proj", p] = project(p)
    run_ref[...] = run
    cnt_ref[...] = run


def _merge(x2, yh, yd, ym, wg_b, wbh_b, wbd_b, wbm_b, wo_b, g1, b1, wr_cat, *, tm, parts):
    n_tok, d = x2.shape
    rp = tm // parts
    tri = (jnp.arange(rp)[:, None] > jnp.arange(rp)[None, :]).astype(BF16)
    row = lambda w: pl.BlockSpec((tm, w), lambda i: (i, 0))
    full = lambda a: pl.BlockSpec(a.shape, lambda i: (0, 0))
    return pl.pallas_call(
        functools.partial(_merge_kernel, parts=parts),
        out_shape=(jax.ShapeDtypeStruct((n_tok, d), F32),
                   jax.ShapeDtypeStruct((n_tok, LANES), F32),
                   jax.ShapeDtypeStruct((SUBLANES, n_tok), F32),
                   jax.ShapeDtypeStruct((1, LANES), F32)),
        grid=(n_tok // tm,),
        in_specs=[row(d), row(yh.shape[1]), row(yd.shape[1]), row(ym.shape[1]),
                  full(wg_b), full(wbh_b), full(wbd_b), full(wbm_b), full(wo_b),
                  full(g1), full(b1), full(wr_cat), full(tri)],
        out_specs=(row(d), row(LANES), pl.BlockSpec((SUBLANES, tm), lambda i: (0, i)),
                   pl.BlockSpec((1, LANES), lambda i: (0, 0))),
        scratch_shapes=[pltpu.VMEM((1, LANES), F32)],
        compiler_params=pltpu.CompilerParams(
            dimension_semantics=("arbitrary",), vmem_limit_bytes=V7X_VMEM_LIMIT),
        name="merge",
    )(x2, yh, yd, ym, wg_b, wbh_b, wbd_b, wbm_b, wo_b, g1, b1, wr_cat, tri)


def _scatter_kernel(pend_ref, psize_ref, nv_ref, dest_ref, h_ref, xs_hbm, dest_smem, zero_ref,
                    row_sem, aux_sem, *, ts, rows):
    cp = pltpu.make_async_copy(dest_ref, dest_smem, aux_sem)
    cp.start()
    cp.wait()
    n_blocks = xs_hbm.shape[0] // rows

    @pl.when(pl.program_id(0) == 0)
    def _():
        zero_ref[...] = jnp.zeros_like(zero_ref)

        def fill(start):
            return pltpu.make_async_copy(zero_ref, xs_hbm.at[pl.ds(start, rows)], aux_sem)

        def fills(act):
            for e in range(N_EXPERTS):
                @pl.when(psize_ref[e] > 0)
                def _():
                    act(fill(pl.multiple_of(pend_ref[e] - rows, rows)))
            for blk in range(n_blocks - N_EXPERTS, n_blocks):
                @pl.when(blk >= nv_ref[0])
                def _():
                    act(fill(blk * rows))

        fills(lambda c: c.start())
        fills(lambda c: c.wait())

    def issue(t, carry):
        for k in range(TOP_K):
            d = dest_smem[0, 0, k * ts + t]
            pltpu.make_async_copy(h_ref.at[pl.ds(t, 1)], xs_hbm.at[pl.ds(d, 1)], row_sem).start()
        return carry

    lax.fori_loop(0, ts, issue, 0, unroll=8)
    pltpu.make_async_copy(xs_hbm.at[pl.ds(0, TOP_K * ts)], xs_hbm.at[pl.ds(0, TOP_K * ts)],
                          row_sem).wait()


def _scatter(padded_end, padded, n_valid, dest3, h, *, ts, rows, n_slots):
    n_tok, d = h.shape
    return pl.pallas_call(
        functools.partial(_scatter_kernel, ts=ts, rows=rows),
        out_shape=jax.ShapeDtypeStruct((n_slots, d), h.dtype),
        grid_spec=pltpu.PrefetchScalarGridSpec(
            num_scalar_prefetch=3,
            grid=(n_tok // ts,),
            in_specs=[pl.BlockSpec((1, 1, TOP_K * ts), lambda i, pe, ps, nv: (i, 0, 0)),
                      pl.BlockSpec((ts, d), lambda i, pe, ps, nv: (i, 0))],
            out_specs=pl.BlockSpec(memory_space=pl.ANY),
            scratch_shapes=[pltpu.SMEM((1, 1, TOP_K * ts), jnp.int32),
                            pltpu.VMEM((rows, d), h.dtype),
                            pltpu.SemaphoreType.DMA, pltpu.SemaphoreType.DMA],
        ),
        compiler_params=pltpu.CompilerParams(
            dimension_semantics=("arbitrary",), vmem_limit_bytes=V7X_VMEM_LIMIT),
        name="scatter",
    )(padded_end, padded, n_valid, dest3, h)


def _experts_kernel(be_ref, nv_ref, x_ref, wg_ref, wu_ref, wd_ref, y_ref, wg_b, wu_b, wd_b):
    j = pl.program_id(0)
    new_expert = (j == 0) | (be_ref[j] != be_ref[jnp.maximum(j - 1, 0)])

    @pl.when(new_expert & (j < nv_ref[0]))
    def _():
        wg_b[...] = wg_ref[...].astype(BF16)
        wu_b[...] = wu_ref[...].astype(BF16)
        wd_b[...] = wd_ref[...].astype(BF16)

    @pl.when(j < nv_ref[0])
    def _():
        half = x_ref.shape[0] // 2
        acts = []
        for p in range(2):
            x = x_ref[p * half:(p + 1) * half, :].astype(BF16)
            g = _dot(x, wg_b[...])
            u = _dot(x, wu_b[...])
            acts.append((g * _sigmoid(g) * u).astype(BF16))
        for p in range(2):
            y_ref[p * half:(p + 1) * half, :] = _dot(acts[p], wd_b[...])

    @pl.when(j >= nv_ref[0])
    def _():
        y_ref[...] = jnp.zeros_like(y_ref)


def _experts(block_expert, n_valid, xs, w_gate, w_up, w_down, *, rows):
    n_slots, d = xs.shape
    de = w_gate.shape[2]
    return pl.pallas_call(
        _experts_kernel,
        out_shape=jax.ShapeDtypeStruct((n_slots, d), F32),
        grid_spec=pltpu.PrefetchScalarGridSpec(
            num_scalar_prefetch=2,
            grid=(n_slots // rows,),
            in_specs=[pl.BlockSpec((rows, d), lambda j, be, nv: (jnp.minimum(j, nv[0] - 1), 0)),
                      pl.BlockSpec((None, d, de), lambda j, be, nv: (be[j], 0, 0)),
                      pl.BlockSpec((None, d, de), lambda j, be, nv: (be[j], 0, 0)),
                      pl.BlockSpec((None, de, d), lambda j, be, nv: (be[j], 0, 0))],
            out_specs=pl.BlockSpec((rows, d), lambda j, be, nv: (j, 0)),
            scratch_shapes=[pltpu.VMEM((d, de), BF16), pltpu.VMEM((d, de), BF16),
                            pltpu.VMEM((de, d), BF16)],
        ),
        compiler_params=pltpu.CompilerParams(
            dimension_semantics=("arbitrary",), vmem_limit_bytes=V7X_VMEM_LIMIT),
        name="experts",
    )(block_expert, n_valid, xs, w_gate, w_up, w_down)


def _combine_kernel(dest_ref, dest_next_ref, h_ref, meta_ref, g2_ref, b2_ref, y_hbm, o_ref,
                    dest_smem0, dest_smem1, ybuf0, ybuf1, row_sem, idx_sem, *, tc):
    i = pl.program_id(0)
    slot = i % 2
    dest_smem = (dest_smem0, dest_smem1)
    ybuf = (ybuf0, ybuf1)

    def start_gathers(idx_ref, s):
        cp = pltpu.make_async_copy(idx_ref, dest_smem[s], idx_sem)
        cp.start()
        cp.wait()

        def issue(t, carry):
            for k in range(TOP_K):
                d = dest_smem[s][0, 0, k * tc + t]
                pltpu.make_async_copy(y_hbm.at[pl.ds(d, 1)], ybuf[s].at[k, pl.ds(t, 1)],
                                      row_sem.at[s]).start()
            return carry

        lax.fori_loop(0, tc, issue, 0, unroll=8)

    @pl.when(i == 0)
    def _():
        start_gathers(dest_ref, 0)

    for s in range(2):
        @pl.when((i + 1 < pl.num_programs(0)) & (slot == 1 - s))
        def _():
            start_gathers(dest_next_ref, s)

    for s in range(2):
        @pl.when(slot == s)
        def _():
            for k in range(TOP_K):
                pltpu.make_async_copy(y_hbm.at[pl.ds(0, tc)], ybuf[s].at[k], row_sem.at[s]).wait()
            meta = meta_ref[...]
            ffn = meta[:, 4:5] * ybuf[s][0] + meta[:, 5:6] * ybuf[s][1]
            o_ref[...] = _layer_norm(DEEPNORM_ALPHA * h_ref[...] + ffn, g2_ref[...], b2_ref[...])


def _combine(dest3, h, meta, g2, b2, y, *, tc):
    n_tok, d = h.shape
    n_steps = n_tok // tc
    return pl.pallas_call(
        functools.partial(_combine_kernel, tc=tc),
        out_shape=jax.ShapeDtypeStruct((n_tok, d), F32),
        grid=(n_steps,),
        in_specs=[pl.BlockSpec((1, 1, TOP_K * tc), lambda i: (i, 0, 0)),
                  pl.BlockSpec((1, 1, TOP_K * tc),
                               lambda i: (jnp.minimum(i + 1, n_steps - 1), 0, 0)),
                  pl.BlockSpec((tc, d), lambda i: (i, 0)),
                  pl.BlockSpec((tc, LANES), lambda i: (i, 0)),
                  pl.BlockSpec(g2.shape, lambda i: (0, 0)),
                  pl.BlockSpec(b2.shape, lambda i: (0, 0)),
                  pl.BlockSpec(memory_space=pl.ANY)],
        out_specs=pl.BlockSpec((tc, d), lambda i: (i, 0)),
        scratch_shapes=[pltpu.SMEM((1, 1, TOP_K * tc), jnp.int32),
                        pltpu.SMEM((1, 1, TOP_K * tc), jnp.int32),
                        pltpu.VMEM((TOP_K, tc, d), F32),
                        pltpu.VMEM((TOP_K, tc, d), F32),
                        pltpu.SemaphoreType.DMA((2,)),
                        pltpu.SemaphoreType.DMA],
        compiler_params=pltpu.CompilerParams(
            dimension_semantics=("arbitrary",), vmem_limit_bytes=V7X_VMEM_LIMIT),
        name="combine",
    )(dest3, dest3, h, meta, g2, b2, y)


def _w_in_column_order():
    hgw = HG_HEADS * HG_DK
    qkw = DF_HEADS * DF_DQK
    base = 4 * hgw
    cols = list(range(base))
    for first in (base, base + 2 * qkw):
        for h in range(DF_HEADS):
            cols += list(range(first + h * DF_DQK, first + (h + 1) * DF_DQK))
            cols += list(range(first + qkw + h * DF_DQK, first + qkw + (h + 1) * DF_DQK))
    total = base + 4 * qkw + DF_HEADS * DF_DV + MEM_HEADS * MEM_DH
    cols += list(range(base + 4 * qkw, total))
    return jnp.asarray(cols, jnp.int32)


def kernel(x, mem, positions, w_in, w_gates, hgrn_lower_bounds, hgrn_norm_gain, diff_lambda_q1, diff_lambda_k1, diff_lambda_q2, diff_lambda_k2, diff_subln_gain, w_mem_kv, w_branch_hgrn, w_branch_diff, w_branch_mem, w_out, ln1_gain, ln1_bias, w_group_router, w_expert_router, w_expert_gate, w_expert_up, w_expert_down, ln2_gain, ln2_bias):
    b, s, d = x.shape
    n_tok = b * s
    t = _tiles(s, n_tok)
    assert w_in.shape[0] == DEPTH
    layer = 0
    lam_init = 0.8 - 0.6 * math.exp(-0.3 * layer)
    hgw = HG_HEADS * HG_DK

    w_in_b = jnp.take(w_in[layer], _w_in_column_order(), axis=1).astype(BF16)
    inv_freq = ROPE_THETA ** (-jnp.arange(0, ROT_DIM, 2, dtype=F32) / ROT_DIM)
    invf8 = inv_freq.reshape(ROT_DIM // 2, 1)
    w_router = jnp.concatenate(
        [w_group_router[layer], w_expert_router[layer],
         jnp.zeros((d, LANES - N_GROUPS - N_EXPERTS), F32)], axis=1)
    wr_hi = w_router.astype(BF16)
    wr_lo = (w_router - wr_hi.astype(F32)).astype(BF16)
    wr_cat = jnp.concatenate([wr_hi, wr_lo], axis=1)

    x2 = x.reshape(n_tok, d)
    pos3 = positions.reshape(n_tok // t["proj_rows"], 1, t["proj_rows"])
    p, logf = _proj(x2, pos3, w_in_b, hgrn_lower_bounds, invf8, tm=t["proj_rows"], layer=layer)
    p3 = p.reshape(b, s, p.shape[1])

    y_hg = _hgrn(p3, logf.reshape(b, s, hgw), hgrn_norm_gain[layer].reshape(1, hgw),
                 rows=t["hgrn_rows"], par=t["hgrn_par"])
    y_df = _diffattn(p3, diff_lambda_q1[layer].reshape(1, -1), diff_lambda_k1[layer].reshape(1, -1),
                     diff_lambda_q2[layer].reshape(1, -1), diff_lambda_k2[layer].reshape(1, -1),
                     diff_subln_gain[layer].reshape(1, -1), tq=t["attn_q"], tk=t["attn_k"], rq=t["attn_rows"],
                     lam_init=lam_init, col0=4 * hgw // LANES)
    y_mem = _memattn(p3, mem, w_mem_kv[layer].astype(BF16), tq=t["mem_q"],
                     col0=7 * hgw // (MEM_HEADS * MEM_DH))

    h, meta, meta_t, counts_f = _merge(
        x2, y_hg.reshape(n_tok, -1), y_df.reshape(n_tok, -1), y_mem.reshape(n_tok, -1),
        w_gates[layer].astype(BF16), w_branch_hgrn[layer].astype(BF16),
        w_branch_diff[layer].astype(BF16), w_branch_mem[layer].astype(BF16),
        w_out[layer].astype(BF16), ln1_gain[layer].reshape(1, d), ln1_bias[layer].reshape(1, d),
        wr_cat, tm=t["merge_rows"], parts=t["merge_parts"])

    rows = t["moe_rows"]
    counts = counts_f[0, N_GROUPS:N_GROUPS + N_EXPERTS].astype(jnp.int32)
    padded = (counts + rows - 1) // rows * rows
    padded_end = jnp.cumsum(padded)
    padded_start = padded_end - padded
    expert_id = meta_t[0:TOP_K].astype(jnp.int32)
    rank = meta_t[TOP_K:2 * TOP_K].astype(jnp.int32)
    dest = rank
    for e in range(N_EXPERTS):
        dest = dest + jnp.where(expert_id == e, padded_start[e], 0)

    def dest_tiles(tile):
        return dest.reshape(TOP_K, n_tok // tile, tile).transpose(1, 0, 2).reshape(
            n_tok // tile, 1, TOP_K * tile)
    n_slots = n_tok * TOP_K + N_EXPERTS * rows
    n_blocks = n_slots // rows
    block_start = jnp.arange(n_blocks, dtype=jnp.int32) * rows
    block_expert = jnp.minimum(
        jnp.sum((padded_end[None, :] <= block_start[:, None]).astype(jnp.int32), axis=1),
        N_EXPERTS - 1)
    n_valid = (padded_end[-1:] // rows).astype(jnp.int32)

    ts = t["scatter_rows"]
    xs = _scatter(padded_end.astype(jnp.int32), padded, n_valid,
                  dest_tiles(ts), h, ts=ts, rows=rows, n_slots=n_slots)
    y = _experts(block_expert, n_valid, xs, w_expert_gate[layer],
                 w_expert_up[layer], w_expert_down[layer], rows=rows)
    tc = t["combine_rows"]
    out = _combine(dest_tiles(tc), h, meta,
                   ln2_gain[layer].reshape(1, d), ln2_bias[layer].reshape(1, d), y, tc=tc)
    return out.reshape(b, s, d)
```
